```python
import math
import jax, jax.numpy as jnp
from jax import lax
import numpy as np

D_MODEL = 1024
BATCH = 8
SEQ = 4096
DEPTH = 1

HEAD_DIM = 64
N_ATTN_HEADS = 8
ATTN_WIDTH = N_ATTN_HEADS * HEAD_DIM
N_GLA_HEADS = 4
GLA_DK = 64
GLA_DV = 128
GLA_K_WIDTH = N_GLA_HEADS * GLA_DK
GLA_V_WIDTH = N_GLA_HEADS * GLA_DV
GLA_GATE_RANK = 16
GLA_TAU = 16.0
GLA_CHUNK = 64
MIX_WIDTH = ATTN_WIDTH + GLA_V_WIDTH
MOBA_BLOCK = 256
MOBA_TOPK = 3
MOBA_Q_CHUNK = 32
IN_WIDTHS = (ATTN_WIDTH, ATTN_WIDTH, ATTN_WIDTH, GLA_K_WIDTH, GLA_K_WIDTH,
             GLA_V_WIDTH, GLA_V_WIDTH, GLA_GATE_RANK)
IN_WIDTH = sum(IN_WIDTHS)
IN_SPLITS = tuple(int(s) for s in np.cumsum(IN_WIDTHS)[:-1])
D_FF = 2816
CONV_WIDTH = 3
ALPHA = (2.0 * DEPTH) ** 0.25
BETA = (8.0 * DEPTH) ** -0.25
EPS = 1e-5
NEG = -1e30

kernel_name = "hymba_moba_gla_convglu_deepnorm_adaln"


def layer_norm(x, g=None, b=None):
    xf = x.astype(jnp.float32)
    mu = jnp.mean(xf, -1, keepdims=True)
    var = jnp.mean(jnp.square(xf - mu), -1, keepdims=True)
    y = (xf - mu) * lax.rsqrt(var + EPS)
    if g is not None:
        y = y * g + b
    return y.astype(x.dtype)


def rms_norm(x, g):
    xf = x.astype(jnp.float32)
    y = xf * lax.rsqrt(jnp.mean(xf * xf, -1, keepdims=True) + EPS) * g
    return y.astype(x.dtype)


def moba_attention(q, k, v):
    B_, H, S, Dh = q.shape
    L = MOBA_BLOCK
    Sp = ((S + L - 1) // L) * L
    pad = ((0, 0), (0, 0), (0, Sp - S), (0, 0))
    q, k, v = jnp.pad(q, pad), jnp.pad(k, pad), jnp.pad(v, pad)
    nb = Sp // L
    topk = min(MOBA_TOPK, nb)
    scale = Dh ** -0.5
    kb = k.reshape(B_, H, nb, L, Dh)
    vb = v.reshape(B_, H, nb, L, Dh)
    k_mean = jnp.mean(kb.astype(jnp.float32), axis=3)
    score = jnp.einsum('bhsd,bhnd->bhsn', q.astype(jnp.float32), k_mean)
    q_blk = jnp.arange(Sp) // L
    past = jnp.arange(nb)[None, :] < q_blk[:, None]
    score = jnp.where(past, score, NEG)
    _, sel = lax.top_k(score, topk)
    sel_valid = jnp.arange(topk)[None, :] < q_blk[:, None]

    QC = MOBA_Q_CHUNK
    n_qc = Sp // QC
    qc = q.reshape(B_, H, n_qc, QC, Dh).transpose(2, 0, 1, 3, 4)
    selc = sel.reshape(B_, H, n_qc, QC, topk).transpose(2, 0, 1, 3, 4)
    validc = sel_valid.reshape(n_qc, QC, topk)
    b_idx = jnp.arange(B_)[:, None, None, None]
    h_idx = jnp.arange(H)[None, :, None, None]

    def step(args):
        i, q_i, sel_i, valid_i = args
        own = (i * QC) // L
        k_own = lax.dynamic_index_in_dim(kb, own, axis=2, keepdims=False)
        v_own = lax.dynamic_index_in_dim(vb, own, axis=2, keepdims=False)
        k_sel = kb[b_idx, h_idx, sel_i]
        v_sel = vb[b_idx, h_idx, sel_i]
        lg_sel = jnp.einsum('bhqd,bhqjld->bhqjl', q_i, k_sel,
                            preferred_element_type=jnp.float32) * scale
        lg_sel = jnp.where(valid_i[None, None, :, :, None], lg_sel, NEG)
        lg_own = jnp.einsum('bhqd,bhld->bhql', q_i, k_own,
                            preferred_element_type=jnp.float32) * scale
        pos_q = i * QC + jnp.arange(QC)
        pos_k = own * L + jnp.arange(L)
        lg_own = jnp.where(pos_k[None, :] <= pos_q[:, None], lg_own, NEG)
        logits = jnp.concatenate([lg_sel.reshape(B_, H, QC, topk * L), lg_own], -1)
        p = jax.nn.softmax(logits, axis=-1).astype(v.dtype)
        p_sel = p[..., :topk * L].reshape(B_, H, QC, topk, L)
        p_own = p[..., topk * L:]
        return (jnp.einsum('bhqjl,bhqjld->bhqd', p_sel, v_sel)
                + jnp.einsum('bhql,bhld->bhqd', p_own, v_own))

    out = lax.map(step, (jnp.arange(n_qc), qc, selc, validc))
    out = out.transpose(1, 2, 0, 3, 4).reshape(B_, H, Sp, Dh)
    return out[:, :, :S]


def gla_attention(q, k, v, log_a):
    B_, H, S, Dk = q.shape
    Dv = v.shape[-1]
    C = GLA_CHUNK
    N = S // C
    f32 = jnp.float32
    q = q.astype(f32).reshape(B_, H, N, C, Dk) * (Dk ** -0.5)
    k = k.astype(f32).reshape(B_, H, N, C, Dk)
    v32 = v.astype(f32).reshape(B_, H, N, C, Dv)
    b = jnp.cumsum(log_a.astype(f32).reshape(B_, H, N, C, Dk), axis=3)
    b_last = b[:, :, :, -1]
    q_dec = q * jnp.exp(b)
    k_dec = k * jnp.exp(-b)
    k_to_end = k * jnp.exp(b_last[:, :, :, None] - b)
    causal = jnp.tril(jnp.ones((C, C), dtype=bool))
    attn = jnp.where(causal, jnp.einsum('bhncd,bhnsd->bhncs', q_dec, k_dec), 0.0)
    o_intra = jnp.einsum('bhncs,bhnsv->bhncv', attn, v32)
    kv = jnp.einsum('bhnsd,bhnsv->bhndv', k_to_end, v32)

    def scan_fn(state, inp):
        decay, kv_n = inp
        return state * decay[..., None] + kv_n, state

    s0 = jnp.zeros((B_, H, Dk, Dv), f32)
    _, states = lax.scan(scan_fn, s0, (jnp.exp(b_last).transpose(2, 0, 1, 3),
                                       kv.transpose(2, 0, 1, 3, 4)))
    states = states.transpose(1, 2, 0, 3, 4)
    o_inter = jnp.einsum('bhncd,bhndv->bhncv', q_dec, states)
    return (o_intra + o_inter).reshape(B_, H, S, Dv).astype(v.dtype)


def token_mixer(u, w_in, w_gla_gate, b_gla_gate, attn_norm_g, gla_norm_g, w_o):
    B_, S, _ = u.shape
    proj = u @ w_in
    qa, ka, va, qg, kg, vg, gg, rg = jnp.split(proj, IN_SPLITS, axis=-1)

    def heads(t, n):
        return t.reshape(B_, S, n, -1).transpose(0, 2, 1, 3)

    o_a = moba_attention(heads(qa, N_ATTN_HEADS), heads(ka, N_ATTN_HEADS), heads(va, N_ATTN_HEADS))
    log_a = jax.nn.log_sigmoid((rg @ w_gla_gate + b_gla_gate).astype(jnp.float32)) / GLA_TAU
    o_g = gla_attention(heads(qg, N_GLA_HEADS), heads(kg, N_GLA_HEADS),
                        heads(vg, N_GLA_HEADS), heads(log_a, N_GLA_HEADS))
    o_a = rms_norm(o_a.transpose(0, 2, 1, 3), attn_norm_g).reshape(B_, S, ATTN_WIDTH)
    o_g = rms_norm(o_g.transpose(0, 2, 1, 3), gla_norm_g).reshape(B_, S, GLA_V_WIDTH)
    o_g = o_g * jax.nn.silu(gg)
    return jnp.concatenate([o_a, o_g], axis=-1) @ w_o


def causal_dwconv(h, w, b):
    out = lax.conv_general_dilated(
        h, w[:, None, :].astype(h.dtype), window_strides=(1,),
        padding=[(CONV_WIDTH - 1, 0)], dimension_numbers=('NWC', 'WIO', 'NWC'),
        feature_group_count=h.shape[-1])
    return out + b


def conv_glu(u, w_up, b_up, conv_w, conv_b, w_down):
    h = u @ w_up + b_up
    hv, hg = jnp.split(h, 2, axis=-1)
    a = jax.nn.gelu(causal_dwconv(hg, conv_w, conv_b), approximate=False) * hv
    return a @ w_down


def setup_inputs(seed: int = 0) -> dict:
    key = jax.random.key(seed)
    ks = jax.random.split(key, 20)
    f32 = jnp.float32
    n = lambda k, s: jax.random.normal(k, s, f32)
    return {
        "x": n(ks[0], (BATCH, SEQ, D_MODEL)),
        "c": n(ks[1], (BATCH, D_MODEL)),
        "w_ada": n(ks[2], (DEPTH, D_MODEL, 6 * D_MODEL)) * D_MODEL ** -0.5,
        "b_ada": n(ks[3], (DEPTH, 6 * D_MODEL)) * 0.01,
        "w_in": n(ks[4], (DEPTH, D_MODEL, IN_WIDTH)) * D_MODEL ** -0.5,
        "w_gla_gate": n(ks[5], (DEPTH, GLA_GATE_RANK, GLA_K_WIDTH)) * GLA_GATE_RANK ** -0.5,
        "b_gla_gate": n(ks[6], (DEPTH, GLA_K_WIDTH)) * 0.1,
        "attn_norm_g": 1.0 + 0.02 * n(ks[7], (DEPTH, N_ATTN_HEADS, HEAD_DIM)),
        "gla_norm_g": 1.0 + 0.02 * n(ks[8], (DEPTH, N_GLA_HEADS, GLA_DV)),
        "w_o": n(ks[9], (DEPTH, MIX_WIDTH, D_MODEL)) * MIX_WIDTH ** -0.5 * BETA,
        "ln1_g": 1.0 + 0.02 * n(ks[10], (DEPTH, D_MODEL)),
        "ln1_b": 0.02 * n(ks[11], (DEPTH, D_MODEL)),
        "w_up": n(ks[12], (DEPTH, D_MODEL, 2 * D_FF)) * D_MODEL ** -0.5,
        "b_up": 0.02 * n(ks[13], (DEPTH, 2 * D_FF)),
        "conv_w": n(ks[14], (DEPTH, CONV_WIDTH, D_FF)) * CONV_WIDTH ** -0.5,
        "conv_b": 0.02 * n(ks[15], (DEPTH, D_FF)),
        "w_down": n(ks[16], (DEPTH, D_FF, D_MODEL)) * D_FF ** -0.5 * BETA,
        "ln2_g": 1.0 + 0.02 * n(ks[17], (DEPTH, D_MODEL)),
        "ln2_b": 0.02 * n(ks[18], (DEPTH, D_MODEL)),
    }


def reference(x, c, w_ada, b_ada, w_in, w_gla_gate, b_gla_gate, attn_norm_g, gla_norm_g,
              w_o, ln1_g, ln1_b, w_up, b_up, conv_w, conv_b, w_down, ln2_g, ln2_b):
    c_act = jax.nn.silu(c)
    for l in range(DEPTH):
        mod = (c_act @ w_ada[l] + b_ada[l])[:, None, :]
        sh1, sc1, g1, sh2, sc2, g2 = jnp.split(mod, 6, axis=-1)
        u = layer_norm(x) * (1.0 + sc1) + sh1
        y = token_mixer(u, w_in[l], w_gla_gate[l], b_gla_gate[l], attn_norm_g[l],
                        gla_norm_g[l], w_o[l])
        x = layer_norm(ALPHA * x + g1 * y, ln1_g[l], ln1_b[l])
        u = layer_norm(x) * (1.0 + sc2) + sh2
        y = conv_glu(u, w_up[l], b_up[l], conv_w[l], conv_b[l], w_down[l])
        x = layer_norm(ALPHA * x + g2 * y, ln2_g[l], ln2_b[l])
    return x
```

```python
import functools

import jax
import jax.numpy as jnp
from jax import lax
from jax.experimental import pallas as pl
from jax.experimental.pallas import tpu as pltpu

F32 = jnp.float32
BF16 = jnp.bfloat16

HEAD_DIM = 64
N_ATTN_HEADS = 8
ATTN_WIDTH = N_ATTN_HEADS * HEAD_DIM
N_GLA_HEADS = 4
GLA_DK = 64
GLA_DV = 128
GLA_K_WIDTH = N_GLA_HEADS * GLA_DK
GLA_V_WIDTH = N_GLA_HEADS * GLA_DV
GLA_GATE_RANK = 16
GLA_TAU = 16.0
GLA_CHUNK = 64
MOBA_BLOCK = 256
MOBA_TOPK = 3
CONV_WIDTH = 3
DEPTH = 1
ALPHA = (2.0 * DEPTH) ** 0.25
EPS = 1e-5
NEG = -1e30

LANES = 128
BF16_ROWS = 16
VMEM_LIMIT = 56 * 1024 * 1024

ROW_TILE = 512
GLA_ROWS = 256
FF_CHUNK = 256
CONV_HALO = BF16_ROWS


def _dot(a, b):
    return jnp.dot(a, b, preferred_element_type=F32)


def _dot_nt(a, b):
    return lax.dot_general(a, b, (((1,), (1,)), ((), ())), preferred_element_type=F32)


def _dot_tn(a, b):
    return lax.dot_general(a, b, (((0,), (0,)), ((), ())), preferred_element_type=F32)


def _split_bf16(a):
    hi = a.astype(BF16)
    lo = (a - hi.astype(F32)).astype(BF16)
    return hi, lo


def _dot3(a, b):
    ah, al = _split_bf16(a)
    bh, bl = _split_bf16(b)
    return _dot(ah, bh) + (_dot(ah, bl) + _dot(al, bh))


def _layer_norm(x):
    mu = jnp.mean(x, -1, keepdims=True)
    xc = x - mu
    var = jnp.mean(xc * xc, -1, keepdims=True)
    return xc * lax.rsqrt(var + EPS)


def _params(*sem):
    return pltpu.CompilerParams(dimension_semantics=sem, vmem_limit_bytes=VMEM_LIMIT)


def _mod_kernel(c_ref, w_ref, b_ref, o_ref):
    o_ref[...] = _dot3(jax.nn.silu(c_ref[...]), w_ref[...]) + b_ref[...]


def _adaln_mod(c, w, b):
    bsz, d = c.shape
    n = w.shape[1]
    return pl.pallas_call(
        _mod_kernel,
        grid=(n // d,),
        in_specs=[
            pl.BlockSpec((bsz, d), lambda j: (0, 0)),
            pl.BlockSpec((d, d), lambda j: (0, j)),
            pl.BlockSpec((1, d), lambda j: (0, j)),
        ],
        out_specs=pl.BlockSpec((bsz, d), lambda j: (0, j)),
        out_shape=jax.ShapeDtypeStruct((bsz, n), F32),
        compiler_params=_params("parallel"),
        name="adaln_mod",
    )(c, w, b.reshape(1, n))


def _inproj_kernel(x_ref, mod_ref, wa_ref, wg_ref, wr_ref, wgate_ref, bgate_ref,
                   qa_ref, ka_ref, va_ref, qg_ref, kg_ref, vg_ref, gg_ref, la_ref):
    mod = mod_ref[...]
    u = (_layer_norm(x_ref[...]) * (1.0 + mod[1:2]) + mod[0:1]).astype(BF16)
    pa = _dot(u, wa_ref[...])
    aw = ATTN_WIDTH
    qa_ref[...] = (pa[:, :aw] * HEAD_DIM ** -0.5).astype(BF16)
    ka_ref[...] = pa[:, aw:2 * aw].astype(BF16)
    va_ref[...] = pa[:, 2 * aw:].astype(BF16)
    pg = _dot(u, wg_ref[...])
    kw, vw = GLA_K_WIDTH, GLA_V_WIDTH
    qg_ref[...] = pg[:, :kw] * GLA_DK ** -0.5
    kg_ref[...] = pg[:, kw:2 * kw]
    vg_ref[...] = pg[:, 2 * kw:2 * kw + vw].astype(BF16)
    gg_ref[...] = pg[:, 2 * kw + vw:].astype(BF16)
    z = _dot3(_dot(u, wr_ref[...]), wgate_ref[...]) + bgate_ref[...]
    la_ref[...] = (jnp.minimum(z, 0.0) - jnp.log1p(jnp.exp(-jnp.abs(z)))) * (1.0 / GLA_TAU)


def _in_proj(x, mod, w_a, w_g, w_r, w_gate, b_gate):
    bsz, s, d = x.shape
    tm = min(ROW_TILE, s)
    row = lambda b, i: (b, i, 0)
    const = lambda b, i: (0, 0)
    out = lambda w, dt: (pl.BlockSpec((None, tm, w), row), jax.ShapeDtypeStruct((bsz, s, w), dt))
    outs = [out(ATTN_WIDTH, BF16)] * 3 + [out(GLA_K_WIDTH, F32)] * 2 + [out(GLA_V_WIDTH, BF16)] * 2 \
        + [out(GLA_K_WIDTH, F32)]
    return pl.pallas_call(
        _inproj_kernel,
        grid=(bsz, s // tm),
        in_specs=[
            pl.BlockSpec((None, tm, d), row),
            pl.BlockSpec((None, 6, d), lambda b, i: (b, 0, 0)),
            pl.BlockSpec(w_a.shape, const),
            pl.BlockSpec(w_g.shape, const),
            pl.BlockSpec(w_r.shape, const),
            pl.BlockSpec(w_gate.shape, const),
            pl.BlockSpec(b_gate.shape, const),
        ],
        out_specs=[o[0] for o in outs],
        out_shape=[o[1] for o in outs],
        compiler_params=_params("parallel", "parallel"),
        name="in_proj",
    )(x, mod, w_a, w_g, w_r, w_gate, b_gate)


def _moba_kernel(q_ref, k_ref, v_ref, g_ref, o_ref, kma_ref, kmb_ref, m_ref, l_ref, acc_ref,
                 *, nb):
    blk = MOBA_BLOCK
    half = LANES // 2
    i = pl.program_id(2)
    lane = lax.broadcasted_iota(jnp.int32, (blk, LANES), 1)
    lane_f = lane.astype(F32)
    head0 = lane < half

    @pl.when(i == 0)
    def _():
        kma_ref[...] = jnp.zeros_like(kma_ref)
        kmb_ref[...] = jnp.zeros_like(kmb_ref)
        for j in range(nb):
            kb = k_ref[pl.ds(j * blk, blk), :].astype(F32)
            mean = jnp.sum(kb, axis=0, keepdims=True) * (1.0 / blk)
            kma_ref[pl.ds(j, 1), :] = mean
            kmb_ref[pl.ds(half + j, 1), :] = mean

    q2 = q_ref[...].astype(F32)
    zero = jnp.zeros_like(q2)
    q_own = (jnp.where(head0, q2, zero).astype(BF16), jnp.where(head0, zero, q2).astype(BF16))

    def choose(qm, km_ref, blk_id):
        kh, kl = _split_bf16(km_ref[...])
        sc = _dot_nt(qm, kh) + _dot_nt(qm, kl)
        past = (blk_id >= 0.0) & (blk_id < i.astype(F32))
        cur = jnp.where(past, sc, NEG)
        sel = jnp.zeros_like(sc)
        for _ in range(MOBA_TOPK):
            mx = jnp.max(cur, axis=-1, keepdims=True)
            first = jnp.min(jnp.where(cur == mx, lane_f, float(LANES)), axis=-1, keepdims=True)
            hit = lane_f == first
            sel = jnp.where(hit & past, 1.0, sel)
            cur = jnp.where(hit, -jnp.inf, cur)
        return jnp.where(sel > 0.0, 0.0, NEG)

    blk0 = lane_f - float(half)
    bias0 = choose(q_own[0], kmb_ref, blk0)
    bias1 = choose(q_own[1], kma_ref, lane_f)
    q_past = (jnp.where(head0, q2, bias0).astype(BF16), jnp.where(head0, bias1, q2).astype(BF16))

    rows = lax.broadcasted_iota(jnp.int32, (blk, blk), 0)
    cols = lax.broadcasted_iota(jnp.int32, (blk, blk), 1)
    causal = cols <= rows
    own = pl.ds(pl.multiple_of(i * blk, blk), blk)
    k_own = k_ref[own, :]
    v_own = v_ref[own, :]
    pv = []
    for h in range(2):
        s = jnp.where(causal, _dot_nt(q_own[h], k_own), NEG)
        m = jnp.max(s, axis=-1, keepdims=True)
        p = jnp.exp(s - m)
        m_ref[h] = m
        l_ref[h] = jnp.sum(p, axis=-1, keepdims=True)
        pv.append(_dot(p.astype(BF16), v_own))
    acc_ref[...] = jnp.where(head0, pv[0], pv[1])

    def past_block(j, carry):
        at = pl.ds(pl.multiple_of(j * blk, blk), blk)
        k2 = k_ref[at, :].astype(F32)
        v_j = v_ref[at, :]
        jf = j.astype(F32)
        one = jnp.ones_like(k2)
        zero_k = jnp.zeros_like(k2)
        k_past = (jnp.where(head0, k2, jnp.where(blk0 == jf, one, zero_k)).astype(BF16),
                  jnp.where(head0, jnp.where(lane_f == jf, one, zero_k), k2).astype(BF16))
        pv, alpha = [], []
        for h in range(2):
            s = _dot_nt(q_past[h], k_past[h])
            m_old = m_ref[h]
            m_new = jnp.maximum(m_old, jnp.max(s, axis=-1, keepdims=True))
            a = jnp.exp(m_old - m_new)
            p = jnp.exp(s - m_new)
            l_ref[h] = a * l_ref[h] + jnp.sum(p, axis=-1, keepdims=True)
            m_ref[h] = m_new
            pv.append(_dot(p.astype(BF16), v_j))
            alpha.append(a)
        acc_ref[...] = (acc_ref[...] * jnp.where(head0, alpha[0], alpha[1])
                        + jnp.where(head0, pv[0], pv[1]))
        return carry

    lax.fori_loop(0, i, past_block, 0)

    o = acc_ref[...] / jnp.where(head0, l_ref[0], l_ref[1])
    o2 = o * o
    zero_o = jnp.zeros_like(o)
    ms0 = jnp.sum(jnp.where(head0, o2, zero_o), axis=-1, keepdims=True) * (1.0 / HEAD_DIM)
    ms1 = jnp.sum(jnp.where(head0, zero_o, o2), axis=-1, keepdims=True) * (1.0 / HEAD_DIM)
    o_ref[...] = (o * lax.rsqrt(jnp.where(head0, ms0, ms1) + EPS) * g_ref[...]).astype(BF16)


def _moba(qa, ka, va, g_pairs):
    bsz, s, w = qa.shape
    blk = MOBA_BLOCK
    assert s % blk == 0 and w % LANES == 0
    nb = s // blk
    assert nb <= LANES // 2
    npairs = w // LANES
    qspec = pl.BlockSpec((None, blk, LANES), lambda b, p, i: (b, i, p))
    kvspec = pl.BlockSpec((None, s, LANES), lambda b, p, i: (b, 0, p))
    return pl.pallas_call(
        functools.partial(_moba_kernel, nb=nb),
        grid=(bsz, npairs, nb),
        in_specs=[qspec, kvspec, kvspec,
                  pl.BlockSpec((None, 1, LANES), lambda b, p, i: (p, 0, 0))],
        out_specs=qspec,
        out_shape=jax.ShapeDtypeStruct((bsz, s, w), BF16),
        scratch_shapes=[
            pltpu.VMEM((LANES, LANES), F32),
            pltpu.VMEM((LANES, LANES), F32),
            pltpu.VMEM((2, blk, 1), F32),
            pltpu.VMEM((2, blk, 1), F32),
            pltpu.VMEM((blk, LANES), F32),
        ],
        compiler_params=_params("parallel", "parallel", "arbitrary"),
        name="moba",
    )(qa, ka, va, g_pairs)


def _gla_kernel(q_ref, k_ref, la_ref, v_ref, gg_ref, g_ref, o_ref, st_ref, *, nchunks):
    c = GLA_CHUNK
    dv = GLA_DV
    half = LANES // 2

    @pl.when(pl.program_id(2) == 0)
    def _():
        st_ref[...] = jnp.zeros_like(st_ref)

    lane = lax.broadcasted_iota(jnp.int32, (c, LANES), 1)
    rows = lax.broadcasted_iota(jnp.int32, (c, c), 0)
    cols = lax.broadcasted_iota(jnp.int32, (c, c), 1)
    tril = cols <= rows
    ones_tril = jnp.where(tril, 1.0, 0.0).astype(BF16)

    for n in range(nchunks):
        at = pl.ds(n * c, c)
        la_hi, la_lo = _split_bf16(la_ref[at, :])
        b = _dot(ones_tril, la_hi) + _dot(ones_tril, la_lo)
        b_last = b[c - 1:c, :]
        q_dec = q_ref[at, :] * jnp.exp(b)
        kk = k_ref[at, :]
        k_dec = (kk * jnp.exp(-b)).astype(BF16)
        k_end = (kk * jnp.exp(b_last - b)).astype(BF16)
        decay = jnp.exp(b_last)
        zero = jnp.zeros_like(q_dec)
        for h in range(2):
            in_head = (lane < half) if h == 0 else (lane >= half)
            qh = jnp.where(in_head, q_dec, zero).astype(BF16)
            attn = jnp.where(tril, _dot_nt(qh, k_dec), 0.0)
            v_h = v_ref[at, h * dv:(h + 1) * dv]
            st = st_ref[h]
            o = _dot(attn.astype(BF16), v_h) + _dot_nt(qh, st.astype(BF16))
            st_ref[h] = st * decay + _dot_tn(v_h, k_end)
            ms = jnp.mean(o * o, axis=-1, keepdims=True)
            gate = gg_ref[at, h * dv:(h + 1) * dv].astype(F32)
            y = o * lax.rsqrt(ms + EPS) * g_ref[:, h * dv:(h + 1) * dv] * jax.nn.silu(gate)
            o_ref[at, h * dv:(h + 1) * dv] = y.astype(BF16)


def _gla(qg, kg, la, vg, gg, g_pairs):
    bsz, s, kw = qg.shape
    vw = vg.shape[-1]
    tr = min(GLA_ROWS, s)
    assert s % tr == 0 and tr % GLA_CHUNK == 0
    npairs = kw // LANES
    kspec = pl.BlockSpec((None, tr, LANES), lambda b, p, t: (b, t, p))
    vspec = pl.BlockSpec((None, tr, 2 * GLA_DV), lambda b, p, t: (b, t, p))
    return pl.pallas_call(
        functools.partial(_gla_kernel, nchunks=tr // GLA_CHUNK),
        grid=(bsz, npairs, s // tr),
        in_specs=[kspec, kspec, kspec, vspec, vspec,
                  pl.BlockSpec((None, 1, 2 * GLA_DV), lambda b, p, t: (p, 0, 0))],
        out_specs=vspec,
        out_shape=jax.ShapeDtypeStruct((bsz, s, vw), BF16),
        scratch_shapes=[pltpu.VMEM((2, GLA_DV, LANES), F32)],
        compiler_params=_params("parallel", "parallel", "arbitrary"),
        name="gla",
    )(qg, kg, la, vg, gg, g_pairs)


def _outproj_kernel(oa_ref, og_ref, x_ref, mod_ref, woa_ref, wog_ref, g_ref, b_ref,
                    x1_ref, u2_ref):
    mod = mod_ref[...]
    y = _dot(oa_ref[...], woa_ref[...]) + _dot(og_ref[...], wog_ref[...])
    x1 = _layer_norm(ALPHA * x_ref[...] + mod[2:3] * y) * g_ref[...] + b_ref[...]
    x1_ref[...] = x1
    u2_ref[...] = (_layer_norm(x1) * (1.0 + mod[4:5]) + mod[3:4]).astype(BF16)


def _out_proj(oa, og, x, mod, w_oa, w_og, ln_g, ln_b):
    bsz, s, d = x.shape
    tm = min(ROW_TILE, s)
    row = lambda b, i: (b, i, 0)
    const = lambda b, i: (0, 0)
    return pl.pallas_call(
        _outproj_kernel,
        grid=(bsz, s // tm),
        in_specs=[
            pl.BlockSpec((None, tm, oa.shape[-1]), row),
            pl.BlockSpec((None, tm, og.shape[-1]), row),
            pl.BlockSpec((None, tm, d), row),
            pl.BlockSpec((None, 6, d), lambda b, i: (b, 0, 0)),
            pl.BlockSpec(w_oa.shape, const),
            pl.BlockSpec(w_og.shape, const),
            pl.BlockSpec((1, d), const),
            pl.BlockSpec((1, d), const),
        ],
        out_specs=[pl.BlockSpec((None, tm, d), row)] * 2,
        out_shape=[jax.ShapeDtypeStruct((bsz, s, d), F32), jax.ShapeDtypeStruct((bsz, s, d), BF16)],
        compiler_params=_params("parallel", "parallel"),
        name="out_proj",
    )(oa, og, x, mod, w_oa, w_og, ln_g, ln_b)


def _ffn_kernel(u_ref, uh_ref, x_ref, mod_ref, wv_ref, wg_ref, bv_ref, bg_ref, cw_ref, cb_ref,
                wd_ref, g_ref, b_ref, o_ref, acc_ref, *, nfc):
    tm = u_ref.shape[0]
    halo = uh_ref.shape[0]
    u = u_ref[...]
    ucat = jnp.concatenate([uh_ref[...], u], axis=0)
    rowid = lax.broadcasted_iota(jnp.int32, (halo + tm, 1), 0)
    keep = (rowid >= halo) | (pl.program_id(1) > 0)
    acc_ref[...] = jnp.zeros_like(acc_ref)

    def chunk(c, carry):
        hv = _dot(u, wv_ref[c]) + bv_ref[c]
        hg = jnp.where(keep, _dot(ucat, wg_ref[c]) + bg_ref[c], 0.0)
        cw = cw_ref[c]
        conv = cb_ref[c]
        for tap in range(CONV_WIDTH):
            back = CONV_WIDTH - 1 - tap
            conv = conv + cw[tap:tap + 1] * hg[halo - back:halo - back + tm]
        a = 0.5 * conv * (1.0 + lax.erf(conv * (0.5 ** 0.5))) * hv
        acc_ref[...] += _dot(a.astype(BF16), wd_ref[c])
        return carry

    lax.fori_loop(0, nfc, chunk, 0)
    mod = mod_ref[...]
    h2 = ALPHA * x_ref[...] + mod[5:6] * acc_ref[...]
    o_ref[...] = _layer_norm(h2) * g_ref[...] + b_ref[...]


def _conv_glu(u2, x1, mod, w_v, w_g, b_v, b_g, conv_w, conv_b, w_d, ln_g, ln_b):
    bsz, s, d = x1.shape
    tm = min(ROW_TILE, s)
    halo = CONV_HALO
    assert tm % halo == 0 and halo >= CONV_WIDTH - 1
    nfc = w_v.shape[0]
    row = lambda b, i: (b, i, 0)
    c2 = lambda b, i: (0, 0)
    c3 = lambda b, i: (0, 0, 0)
    return pl.pallas_call(
        functools.partial(_ffn_kernel, nfc=nfc),
        grid=(bsz, s // tm),
        in_specs=[
            pl.BlockSpec((None, tm, d), row),
            pl.BlockSpec((None, halo, d), lambda b, i: (b, jnp.maximum(i * (tm // halo) - 1, 0), 0)),
            pl.BlockSpec((None, tm, d), row),
            pl.BlockSpec((None, 6, d), lambda b, i: (b, 0, 0)),
            pl.BlockSpec(w_v.shape, c3),
            pl.BlockSpec(w_g.shape, c3),
            pl.BlockSpec(b_v.shape, c3),
            pl.BlockSpec(b_g.shape, c3),
            pl.BlockSpec(conv_w.shape, c3),
            pl.BlockSpec(conv_b.shape, c3),
            pl.BlockSpec(w_d.shape, c3),
            pl.BlockSpec((1, d), c2),
            pl.BlockSpec((1, d), c2),
        ],
        out_specs=pl.BlockSpec((None, tm, d), row),
        out_shape=jax.ShapeDtypeStruct((bsz, s, d), F32),
        scratch_shapes=[pltpu.VMEM((tm, d), F32)],
        compiler_params=_params("parallel", "parallel"),
        name="conv_glu",
    )(u2, u2, x1, mod, w_v, w_g, b_v, b_g, conv_w, conv_b, w_d, ln_g, ln_b)


def _chunk_cols(w, fc):
    k, f = w.shape
    return w.reshape(k, f // fc, fc).transpose(1, 0, 2)


def _layer(x, mod, w_in, w_gla_gate, b_gla_gate, attn_norm_g, gla_norm_g, w_o, ln1_g, ln1_b,
           w_up, b_up, conv_w, conv_b, w_down, ln2_g, ln2_b):
    d = x.shape[-1]
    d_ff = w_down.shape[0]
    a3 = 3 * ATTN_WIDTH
    g_end = a3 + 2 * GLA_K_WIDTH + 2 * GLA_V_WIDTH
    w_a = w_in[:, :a3].astype(BF16)
    w_g = w_in[:, a3:g_end].astype(BF16)
    w_r = jnp.pad(w_in[:, g_end:], ((0, 0), (0, LANES - GLA_GATE_RANK))).astype(BF16)
    w_gate = jnp.pad(w_gla_gate, ((0, LANES - GLA_GATE_RANK), (0, 0)))
    qa, ka, va, qg, kg, vg, gg, la = _in_proj(x, mod, w_a, w_g, w_r, w_gate,
                                              b_gla_gate.reshape(1, -1))
    oa = _moba(qa, ka, va, attn_norm_g.reshape(-1, 1, LANES))
    og = _gla(qg, kg, la, vg, gg, gla_norm_g.reshape(-1, 1, 2 * GLA_DV))
    w_ob = w_o.astype(BF16)
    x1, u2 = _out_proj(oa, og, x, mod, w_ob[:ATTN_WIDTH], w_ob[ATTN_WIDTH:],
                       ln1_g.reshape(1, d), ln1_b.reshape(1, d))
    fc = FF_CHUNK
    nfc = d_ff // fc
    w_upb = w_up.astype(BF16)
    return _conv_glu(
        u2, x1, mod,
        _chunk_cols(w_upb[:, :d_ff], fc), _chunk_cols(w_upb[:, d_ff:], fc),
        b_up[:d_ff].reshape(nfc, 1, fc), b_up[d_ff:].reshape(nfc, 1, fc),
        _chunk_cols(conv_w, fc), conv_b.reshape(nfc, 1, fc),
        w_down.astype(BF16).reshape(nfc, fc, d),
        ln2_g.reshape(1, d), ln2_b.reshape(1, d))


def kernel(x, c, w_ada, b_ada, w_in, w_gla_gate, b_gla_gate, attn_norm_g, gla_norm_g, w_o,
           ln1_g, ln1_b, w_up, b_up, conv_w, conv_b, w_down, ln2_g, ln2_b):
    bsz, _, d = x.shape
    for l in range(w_in.shape[0]):
        mod = _adaln_mod(c, w_ada[l], b_ada[l]).reshape(bsz, 6, d)
        x = _layer(x, mod, w_in[l], w_gla_gate[l], b_gla_gate[l], attn_norm_g[l], gla_norm_g[l],
                   w_o[l], ln1_g[l], ln1_b[l], w_up[l], b_up[l], conv_w[l], conv_b[l], w_down[l],
                   ln2_g[l], ln2_b[l])
    return x
```

```python
import functools

import jax
import jax.numpy as jnp
from jax import lax
from jax.experimental import pallas as pl
from jax.experimental.pallas import tpu as pltpu

F32 = jnp.float32
BF16 = jnp.bfloat16

HEAD_DIM = 64
N_ATTN_HEADS = 8
ATTN_WIDTH = N_ATTN_HEADS * HEAD_DIM
N_GLA_HEADS = 4
GLA_DK = 64
GLA_DV = 128
GLA_K_WIDTH = N_GLA_HEADS * GLA_DK
GLA_V_WIDTH = N_GLA_HEADS * GLA_DV
GLA_GATE_RANK = 16
GLA_TAU = 16.0
GLA_CHUNK = 64
MOBA_BLOCK = 256
MOBA_TOPK = 3
CONV_WIDTH = 3
DEPTH = 1
ALPHA = (2.0 * DEPTH) ** 0.25
EPS = 1e-5
NEG = -1e30
LOG2E = 1.4426950408889634

LANES = 128
BF16_ROWS = 16
VMEM_LIMIT = 56 * 1024 * 1024

ROW_TILE = 512
GLA_ROWS = 512
GLA_SUM_ROWS = 256
FF_CHUNK = 256
CONV_HALO = BF16_ROWS


def _dot(a, b):
    return jnp.dot(a, b, preferred_element_type=F32)


def _dot_nt(a, b):
    return lax.dot_general(a, b, (((1,), (1,)), ((), ())), preferred_element_type=F32)


def _dot_tn(a, b):
    return lax.dot_general(a, b, (((0,), (0,)), ((), ())), preferred_element_type=F32)


def _split_bf16(a):
    hi = a.astype(BF16)
    lo = (a - hi.astype(F32)).astype(BF16)
    return hi, lo


def _dot3(a, b):
    ah, al = _split_bf16(a)
    bh, bl = _split_bf16(b)
    return _dot(ah, bh) + (_dot(ah, bl) + _dot(al, bh))


def _layer_norm(x):
    mu = jnp.mean(x, -1, keepdims=True)
    xc = x - mu
    var = jnp.mean(xc * xc, -1, keepdims=True)
    return xc * lax.rsqrt(var + EPS)


def _params(*sem):
    return pltpu.CompilerParams(dimension_semantics=sem, vmem_limit_bytes=VMEM_LIMIT)


def _mod_kernel(c_ref, w_ref, b_ref, o_ref):
    o_ref[...] = _dot3(jax.nn.silu(c_ref[...]), w_ref[...]) + b_ref[...]


def _adaln_mod(c, w, b):
    bsz, d = c.shape
    n = w.shape[1]
    return pl.pallas_call(
        _mod_kernel,
        grid=(n // d,),
        in_specs=[
            pl.BlockSpec((bsz, d), lambda j: (0, 0)),
            pl.BlockSpec((d, d), lambda j: (0, j)),
            pl.BlockSpec((1, d), lambda j: (0, j)),
        ],
        out_specs=pl.BlockSpec((bsz, d), lambda j: (0, j)),
        out_shape=jax.ShapeDtypeStruct((bsz, n), F32),
        compiler_params=_params("parallel"),
        name="adaln_mod",
    )(c, w, b.reshape(1, n))


def _inproj_kernel(x_ref, mod_ref, wa_ref, wg_ref, wr_ref, wgate_ref, bgate_ref,
                   qa_ref, ka_ref, va_ref, qg_ref, kg_ref, vg_ref, gg_ref, la_ref):
    mod = mod_ref[...]
    u = (_layer_norm(x_ref[...]) * (1.0 + mod[1:2]) + mod[0:1]).astype(BF16)
    pa = _dot(u, wa_ref[...])
    aw = ATTN_WIDTH
    qa_ref[...] = (pa[:, :aw] * (HEAD_DIM ** -0.5 * LOG2E)).astype(BF16)
    ka_ref[...] = pa[:, aw:2 * aw].astype(BF16)
    va_ref[...] = pa[:, 2 * aw:].astype(BF16)
    pg = _dot(u, wg_ref[...])
    kw, vw = GLA_K_WIDTH, GLA_V_WIDTH
    qg_ref[...] = pg[:, :kw] * GLA_DK ** -0.5
    kg_ref[...] = pg[:, kw:2 * kw]
    vg_ref[...] = pg[:, 2 * kw:2 * kw + vw].astype(BF16)
    gg_ref[...] = pg[:, 2 * kw + vw:].astype(BF16)
    z = _dot3(_dot(u, wr_ref[...]), wgate_ref[...]) + bgate_ref[...]
    la_ref[...] = (jnp.minimum(z, 0.0) - jnp.log1p(jnp.exp(-jnp.abs(z)))) * (1.0 / GLA_TAU)


def _in_proj(x, mod, w_a, w_g, w_r, w_gate, b_gate):
    bsz, s, d = x.shape
    tm = min(ROW_TILE, s)
    row = lambda b, i: (b, i, 0)
    const = lambda b, i: (0, 0)
    out = lambda w, dt: (pl.BlockSpec((None, tm, w), row), jax.ShapeDtypeStruct((bsz, s, w), dt))
    outs = [out(ATTN_WIDTH, BF16)] * 3 + [out(GLA_K_WIDTH, F32)] * 2 + [out(GLA_V_WIDTH, BF16)] * 2 \
        + [out(GLA_K_WIDTH, F32)]
    return pl.pallas_call(
        _inproj_kernel,
        grid=(bsz, s // tm),
        in_specs=[
            pl.BlockSpec((None, tm, d), row),
            pl.BlockSpec((None, 6, d), lambda b, i: (b, 0, 0)),
            pl.BlockSpec(w_a.shape, const),
            pl.BlockSpec(w_g.shape, const),
            pl.BlockSpec(w_r.shape, const),
            pl.BlockSpec(w_gate.shape, const),
            pl.BlockSpec(b_gate.shape, const),
        ],
        out_specs=[o[0] for o in outs],
        out_shape=[o[1] for o in outs],
        compiler_params=_params("parallel", "parallel"),
        name="in_proj",
    )(x, mod, w_a, w_g, w_r, w_gate, b_gate)


def _fold_lanes(x, op):
    parts = [x[:, t * LANES:(t + 1) * LANES] for t in range(x.shape[1] // LANES)]
    while len(parts) > 1:
        parts = [op(parts[t], parts[t + 1]) for t in range(0, len(parts), 2)]
    return parts[0]


def _moba_kernel(q_ref, k_ref, v_ref, g_ref, o_ref, km_ref, kid_ref, vone_ref, s_ref, sda_ref,
                 sdb_ref, mrun_ref, acc_ref, *, nb):
    blk = MOBA_BLOCK
    tq = 2 * blk
    half = LANES // 2
    nbp = km_ref.shape[0]
    t = pl.program_id(2)
    first_blk = (2 * t).astype(F32)
    head0 = lax.broadcasted_iota(jnp.int32, (tq, LANES), 1) < half

    @pl.when(t == 0)
    def _():
        km_ref[...] = jnp.zeros_like(km_ref)
        lane_k = lax.broadcasted_iota(jnp.int32, (blk, LANES), 1)
        h0 = lane_k < half
        one = jnp.ones((blk, LANES), F32)
        zero_k = jnp.zeros((blk, LANES), F32)
        for j in range(nb):
            at = pl.ds(j * blk, blk)
            kb = k_ref[at, :].astype(F32)
            vb = v_ref[at, :].astype(F32)
            km_ref[pl.ds(j, 1), :] = jnp.sum(kb, axis=0, keepdims=True) * (1.0 / blk)
            kid_ref[0, at, :] = jnp.where(h0, kb, jnp.where(lane_k == half + j, one, zero_k)).astype(BF16)
            kid_ref[1, at, :] = jnp.where(h0, jnp.where(lane_k == j, one, zero_k), kb).astype(BF16)
            vone_ref[0, at, :] = jnp.where(h0, vb, one).astype(BF16)
            vone_ref[1, at, :] = jnp.where(h0, one, vb).astype(BF16)

    q2 = q_ref[...].astype(F32)
    zero = jnp.zeros_like(q2)
    q_own = (jnp.where(head0, q2, zero).astype(BF16), jnp.where(head0, zero, q2).astype(BF16))

    km_hi, km_lo = _split_bf16(km_ref[...])
    blk_id = lax.broadcasted_iota(jnp.int32, (nbp, tq), 0).astype(F32)
    q_blk = first_blk + jnp.where(lax.broadcasted_iota(jnp.int32, (nbp, tq), 1) >= blk, 1.0, 0.0)
    past = blk_id < q_blk

    def choose(qm):
        cur = jnp.where(past, _dot_nt(km_hi, qm) + _dot_nt(km_lo, qm), NEG)
        sel = jnp.zeros_like(cur)
        for _ in range(MOBA_TOPK):
            mx = jnp.max(cur, axis=0, keepdims=True)
            first = jnp.min(jnp.where(cur == mx, blk_id, float(nbp)), axis=0, keepdims=True)
            hit = blk_id == first
            sel = jnp.where(hit & past, 1.0, sel)
            cur = jnp.where(hit, -jnp.inf, cur)
        return jnp.where(sel > 0.0, 0.0, NEG)

    fill = jnp.zeros((half - nbp, tq), F32)
    bias = jnp.concatenate([choose(q_own[1]), fill, choose(q_own[0]), fill], axis=0).T
    q_past = (jnp.where(head0, q2, bias).astype(BF16), jnp.where(head0, bias, q2).astype(BF16))

    rows = lax.broadcasted_iota(jnp.int32, (blk, blk), 0)
    cols = lax.broadcasted_iota(jnp.int32, (blk, blk), 1)
    causal = cols <= rows
    row_a = pl.ds(pl.multiple_of(t * tq, tq), blk)
    row_b = pl.ds(pl.multiple_of(t * tq, tq) + blk, blk)
    own = pl.ds(pl.multiple_of(t * tq, tq), tq)
    halves = (pl.ds(0, blk), pl.ds(blk, blk))
    k_a = k_ref[row_a, :]
    k_b = k_ref[row_b, :]
    for h in range(2):
        s_a = jnp.where(causal, _dot_nt(q_own[h][:blk], k_a), NEG)
        s_b = jnp.concatenate([_dot_nt(q_past[h][blk:], kid_ref[h, row_a, :]),
                               jnp.where(causal, _dot_nt(q_own[h][blk:], k_b), NEG)], axis=1)
        sda_ref[h] = s_a
        sdb_ref[h] = s_b
        mrun_ref[h, halves[0], :] = _fold_lanes(s_a, jnp.maximum)
        mrun_ref[h, halves[1], :] = _fold_lanes(s_b, jnp.maximum)

    def score_pair(jj, carry):
        at = pl.ds(pl.multiple_of(jj * tq, tq), tq)
        for h in range(2):
            s = _dot_nt(q_past[h], kid_ref[h, at, :])
            s_ref[h, jj] = s
            mrun_ref[h] = jnp.maximum(mrun_ref[h], _fold_lanes(s, jnp.maximum))
        return carry

    lax.fori_loop(0, t, score_pair, 0)

    m = [[jnp.max(mrun_ref[h, r, :], axis=-1, keepdims=True) for r in halves] for h in range(2)]
    for h in range(2):
        p_a = jnp.exp2(sda_ref[h] - m[h][0])
        p_b = jnp.exp2(sdb_ref[h] - m[h][1])
        acc_ref[h, halves[0], :] = _dot(p_a.astype(BF16), vone_ref[h, row_a, :])
        acc_ref[h, halves[1], :] = _dot(p_b.astype(BF16), vone_ref[h, own, :])

    def pv_pair(jj, carry):
        at = pl.ds(pl.multiple_of(jj * tq, tq), tq)
        for h in range(2):
            p = jnp.concatenate(
                [jnp.exp2(s_ref[h, jj, r, :] - m[h][e]) for e, r in enumerate(halves)], axis=0)
            acc_ref[h] += _dot(p.astype(BF16), vone_ref[h, at, :])
        return carry

    lax.fori_loop(0, t, pv_pair, 0)

    acc = (acc_ref[0], acc_ref[1])
    row_sum = pltpu.roll(jnp.where(head0, acc[1], acc[0]), half, 1)
    o = jnp.where(head0, acc[0], acc[1]) / row_sum
    same_head = ((lax.broadcasted_iota(jnp.int32, (LANES, LANES), 0) < half)
                 == (lax.broadcasted_iota(jnp.int32, (LANES, LANES), 1) < half))
    avg = jnp.where(same_head, 1.0 / HEAD_DIM, 0.0).astype(BF16)
    sq_hi, sq_lo = _split_bf16(o * o)
    ms = _dot(sq_hi, avg) + _dot(sq_lo, avg)
    o_ref[...] = (o * lax.rsqrt(ms + EPS) * g_ref[...]).astype(BF16)


def _moba(qa, ka, va, g_pairs):
    bsz, s, w = qa.shape
    blk = MOBA_BLOCK
    tq = 2 * blk
    assert s % tq == 0 and w % LANES == 0
    nb = s // blk
    nbp = -(-nb // 8) * 8
    assert nbp <= LANES // 2
    npairs = w // LANES
    qspec = pl.BlockSpec((None, tq, LANES), lambda b, p, t: (b, t, p))
    kvspec = pl.BlockSpec((None, s, LANES), lambda b, p, t: (b, 0, p))
    return pl.pallas_call(
        functools.partial(_moba_kernel, nb=nb),
        grid=(bsz, npairs, s // tq),
        in_specs=[qspec, kvspec, kvspec,
                  pl.BlockSpec((None, 1, LANES), lambda b, p, t: (p, 0, 0))],
        out_specs=qspec,
        out_shape=jax.ShapeDtypeStruct((bsz, s, w), BF16),
        scratch_shapes=[
            pltpu.VMEM((nbp, LANES), F32),
            pltpu.VMEM((2, s, LANES), BF16),
            pltpu.VMEM((2, s, LANES), BF16),
            pltpu.VMEM((2, max(s // tq - 1, 1), tq, tq), F32),
            pltpu.VMEM((2, blk, blk), F32),
            pltpu.VMEM((2, blk, tq), F32),
            pltpu.VMEM((2, tq, LANES), F32),
            pltpu.VMEM((2, tq, LANES), F32),
        ],
        compiler_params=_params("parallel", "parallel", "arbitrary"),
        name="moba",
    )(qa, ka, va, g_pairs)


def _gla_kernel(q_ref, k_ref, la_ref, v_ref, gg_ref, g_ref, o_ref, st_ref):
    c = GLA_CHUNK
    dv = GLA_DV
    half = LANES // 2
    tr = q_ref.shape[0]
    shift = c.bit_length() - 1

    @pl.when(pl.program_id(2) == 0)
    def _():
        st_ref[...] = jnp.zeros_like(st_ref)

    grp = min(tr, GLA_SUM_ROWS)
    rows = lax.broadcasted_iota(jnp.int32, (grp, grp), 0)
    cols = lax.broadcasted_iota(jnp.int32, (grp, grp), 1)
    same_chunk = lax.shift_right_logical(rows, shift) == lax.shift_right_logical(cols, shift)
    upto = jnp.where(same_chunk & (cols <= rows), 1.0, 0.0).astype(BF16)
    whole = jnp.where(same_chunk, 1.0, 0.0).astype(BF16)
    b, b_end = [], []
    for g0 in range(0, tr, grp):
        la_hi, la_lo = _split_bf16(la_ref[pl.ds(g0, grp), :])
        b.append(_dot(upto, la_hi) + _dot(upto, la_lo))
        b_end.append(_dot(whole, la_hi) + _dot(whole, la_lo))
    b = jnp.concatenate(b, axis=0)
    b_end = jnp.concatenate(b_end, axis=0)
    kk = k_ref[...]
    q_dec = q_ref[...] * jnp.exp(b)
    k_dec = (kk * jnp.exp(-b)).astype(BF16)
    k_end = (kk * jnp.exp(b_end - b)).astype(BF16)
    decay = jnp.exp(b_end)
    head0 = lax.broadcasted_iota(jnp.int32, (tr, LANES), 1) < half
    zero = jnp.zeros_like(q_dec)
    q_h0 = jnp.where(head0, q_dec, zero).astype(BF16)
    q_h1 = jnp.where(head0, zero, q_dec).astype(BF16)
    in_chunk_row = lax.broadcasted_iota(jnp.int32, (2 * c, c), 0) & (c - 1)
    causal = lax.broadcasted_iota(jnp.int32, (2 * c, c), 1) <= in_chunk_row

    st = st_ref[...]
    outs = []
    for n in range(tr // c):
        at = slice(n * c, (n + 1) * c)
        qs = jnp.concatenate([q_h0[at], q_h1[at]], axis=0)
        attn = jnp.where(causal, _dot_nt(qs, k_dec[at]), 0.0)
        v_c = v_ref[pl.ds(n * c, c), :]
        res = _dot(attn.astype(BF16), v_c) + _dot_nt(qs, st.astype(BF16))
        outs.append(jnp.concatenate([res[:c, :dv], res[c:, dv:]], axis=1))
        st = st * decay[n * c:n * c + 1, :] + _dot_tn(v_c, k_end[at])
    st_ref[...] = st
    o = jnp.concatenate(outs, axis=0)

    same_head = ((lax.broadcasted_iota(jnp.int32, (2 * dv, 2 * dv), 0) < dv)
                 == (lax.broadcasted_iota(jnp.int32, (2 * dv, 2 * dv), 1) < dv))
    avg = jnp.where(same_head, 1.0 / dv, 0.0).astype(BF16)
    sq_hi, sq_lo = _split_bf16(o * o)
    ms = _dot(sq_hi, avg) + _dot(sq_lo, avg)
    y = o * lax.rsqrt(ms + EPS) * g_ref[...] * jax.nn.silu(gg_ref[...].astype(F32))
    o_ref[...] = y.astype(BF16)


def _gla(qg, kg, la, vg, gg, g_pairs):
    bsz, s, kw = qg.shape
    vw = vg.shape[-1]
    tr = min(GLA_ROWS, s)
    assert s % tr == 0 and tr % GLA_CHUNK == 0 and GLA_CHUNK & (GLA_CHUNK - 1) == 0
    npairs = kw // LANES
    kspec = pl.BlockSpec((None, tr, LANES), lambda b, p, t: (b, t, p))
    vspec = pl.BlockSpec((None, tr, 2 * GLA_DV), lambda b, p, t: (b, t, p))
    return pl.pallas_call(
        _gla_kernel,
        grid=(bsz, npairs, s // tr),
        in_specs=[kspec, kspec, kspec, vspec, vspec,
                  pl.BlockSpec((None, 1, 2 * GLA_DV), lambda b, p, t: (p, 0, 0))],
        out_specs=vspec,
        out_shape=jax.ShapeDtypeStruct((bsz, s, vw), BF16),
        scratch_shapes=[pltpu.VMEM((2 * GLA_DV, LANES), F32)],
        compiler_params=_params("parallel", "parallel", "arbitrary"),
        name="gla",
    )(qg, kg, la, vg, gg, g_pairs)


def _outproj_kernel(oa_ref, og_ref, x_ref, mod_ref, woa_ref, wog_ref, g_ref, b_ref,
                    x1_ref, u2_ref):
    mod = mod_ref[...]
    y = _dot(oa_ref[...], woa_ref[...]) + _dot(og_ref[...], wog_ref[...])
    x1 = _layer_norm(ALPHA * x_ref[...] + mod[2:3] * y) * g_ref[...] + b_ref[...]
    x1_ref[...] = x1
    u2_ref[...] = (_layer_norm(x1) * (1.0 + mod[4:5]) + mod[3:4]).astype(BF16)


def _out_proj(oa, og, x, mod, w_oa, w_og, ln_g, ln_b):
    bsz, s, d = x.shape
    tm = min(ROW_TILE, s)
    row = lambda b, i: (b, i, 0)
    const = lambda b, i: (0, 0)
    return pl.pallas_call(
        _outproj_kernel,
        grid=(bsz, s // tm),
        in_specs=[
            pl.BlockSpec((None, tm, oa.shape[-1]), row),
            pl.BlockSpec((None, tm, og.shape[-1]), row),
            pl.BlockSpec((None, tm, d), row),
            pl.BlockSpec((None, 6, d), lambda b, i: (b, 0, 0)),
            pl.BlockSpec(w_oa.shape, const),
            pl.BlockSpec(w_og.shape, const),
            pl.BlockSpec((1, d), const),
            pl.BlockSpec((1, d), const),
        ],
        out_specs=[pl.BlockSpec((None, tm, d), row)] * 2,
        out_shape=[jax.ShapeDtypeStruct((bsz, s, d), F32), jax.ShapeDtypeStruct((bsz, s, d), BF16)],
        compiler_params=_params("parallel", "parallel"),
        name="out_proj",
    )(oa, og, x, mod, w_oa, w_og, ln_g, ln_b)


def _ffn_kernel(u_ref, uh_ref, x_ref, mod_ref, wup_ref, bup_ref, cw_ref, cb_ref, wd_ref, g_ref,
                b_ref, o_ref, a_ref):
    tm = u_ref.shape[0]
    halo = uh_ref.shape[0]
    d_ff = wd_ref.shape[0]
    u = u_ref[...]
    ucat = jnp.concatenate([uh_ref[...], u], axis=0)
    rowid = lax.broadcasted_iota(jnp.int32, (halo + tm, 1), 0)
    keep = (rowid >= halo) | (pl.program_id(1) > 0)
    for c0 in range(0, d_ff, FF_CHUNK):
        val = slice(c0, c0 + FF_CHUNK)
        gate = slice(d_ff + c0, d_ff + c0 + FF_CHUNK)
        hv = _dot(u, wup_ref[:, val]) + bup_ref[:, val]
        hg = jnp.where(keep, _dot(ucat, wup_ref[:, gate]) + bup_ref[:, gate], 0.0)
        conv = cb_ref[:, val]
        for tap in range(CONV_WIDTH):
            back = CONV_WIDTH - 1 - tap
            conv = conv + cw_ref[tap:tap + 1, val] * hg[halo - back:halo - back + tm]
        a_ref[:, val] = (0.5 * conv * (1.0 + lax.erf(conv * (0.5 ** 0.5))) * hv).astype(BF16)
    mod = mod_ref[...]
    h2 = ALPHA * x_ref[...] + mod[5:6] * _dot(a_ref[...], wd_ref[...])
    o_ref[...] = _layer_norm(h2) * g_ref[...] + b_ref[...]


def _conv_glu(u2, x1, mod, w_up, b_up, conv_w, conv_b, w_d, ln_g, ln_b):
    bsz, s, d = x1.shape
    d_ff = w_d.shape[0]
    tm = min(ROW_TILE, s)
    halo = CONV_HALO
    assert tm % halo == 0 and halo >= CONV_WIDTH - 1 and d_ff % FF_CHUNK == 0
    row = lambda b, i: (b, i, 0)
    const = lambda b, i: (0, 0)
    return pl.pallas_call(
        _ffn_kernel,
        grid=(bsz, s // tm),
        in_specs=[
            pl.BlockSpec((None, tm, d), row),
            pl.BlockSpec((None, halo, d), lambda b, i: (b, jnp.maximum(i * (tm // halo) - 1, 0), 0)),
            pl.BlockSpec((None, tm, d), row),
            pl.BlockSpec((None, 6, d), lambda b, i: (b, 0, 0)),
            pl.BlockSpec(w_up.shape, const),
            pl.BlockSpec(b_up.shape, const),
            pl.BlockSpec(conv_w.shape, const),
            pl.BlockSpec(conv_b.shape, const),
            pl.BlockSpec(w_d.shape, const),
            pl.BlockSpec((1, d), const),
            pl.BlockSpec((1, d), const),
        ],
        out_specs=pl.BlockSpec((None, tm, d), row),
        out_shape=jax.ShapeDtypeStruct((bsz, s, d), F32),
        scratch_shapes=[pltpu.VMEM((tm, d_ff), BF16)],
        compiler_params=_params("parallel", "parallel"),
        name="conv_glu",
    )(u2, u2, x1, mod, w_up, b_up, conv_w, conv_b, w_d, ln_g, ln_b)


def _layer(x, mod, w_in, w_gla_gate, b_gla_gate, attn_norm_g, gla_norm_g, w_o, ln1_g, ln1_b,
           w_up, b_up, conv_w, conv_b, w_down, ln2_g, ln2_b):
    d = x.shape[-1]
    d_ff = w_down.shape[0]
    a3 = 3 * ATTN_WIDTH
    g_end = a3 + 2 * GLA_K_WIDTH + 2 * GLA_V_WIDTH
    w_a = w_in[:, :a3].astype(BF16)
    w_g = w_in[:, a3:g_end].astype(BF16)
    w_r = jnp.pad(w_in[:, g_end:], ((0, 0), (0, LANES - GLA_GATE_RANK))).astype(BF16)
    w_gate = jnp.pad(w_gla_gate, ((0, LANES - GLA_GATE_RANK), (0, 0)))
    qa, ka, va, qg, kg, vg, gg, la = _in_proj(x, mod, w_a, w_g, w_r, w_gate,
                                              b_gla_gate.reshape(1, -1))
    oa = _moba(qa, ka, va, attn_norm_g.reshape(-1, 1, LANES))
    og = _gla(qg, kg, la, vg, gg, gla_norm_g.reshape(-1, 1, 2 * GLA_DV))
    w_ob = w_o.astype(BF16)
    x1, u2 = _out_proj(oa, og, x, mod, w_ob[:ATTN_WIDTH], w_ob[ATTN_WIDTH:],
                       ln1_g.reshape(1, d), ln1_b.reshape(1, d))
    return _conv_glu(u2, x1, mod, w_up.astype(BF16), b_up.reshape(1, -1), conv_w,
                     conv_b.reshape(1, -1), w_down.astype(BF16),
                     ln2_g.reshape(1, d), ln2_b.reshape(1, d))


def kernel(x, c, w_ada, b_ada, w_in, w_gla_gate, b_gla_gate, attn_norm_g, gla_norm_g, w_o,
           ln1_g, ln1_b, w_up, b_up, conv_w, conv_b, w_down, ln2_g, ln2_b):
    bsz, _, d = x.shape
    for l in range(w_in.shape[0]):
        mod = _adaln_mod(c, w_ada[l], b_ada[l]).reshape(bsz, 6, d)
        x = _layer(x, mod, w_in[l], w_gla_gate[l], b_gla_gate[l], attn_norm_g[l], gla_norm_g[l],
                   w_o[l], ln1_g[l], ln1_b[l], w_up[l], b_up[l], conv_w[l], conv_b[l], w_down[l],
                   ln2_g[l], ln2_b[l])
    return x
```

```python
import functools

import jax
import jax.numpy as jnp
from jax import lax
from jax.experimental import pallas as pl
from jax.experimental.pallas import tpu as pltpu

F32 = jnp.float32
BF16 = jnp.bfloat16

HEAD_DIM = 64
N_ATTN_HEADS = 8
ATTN_WIDTH = N_ATTN_HEADS * HEAD_DIM
N_GLA_HEADS = 4
GLA_DK = 64
GLA_DV = 128
GLA_K_WIDTH = N_GLA_HEADS * GLA_DK
GLA_V_WIDTH = N_GLA_HEADS * GLA_DV
GLA_GATE_RANK = 16
GLA_TAU = 16.0
GLA_CHUNK = 64
MOBA_BLOCK = 256
MOBA_TOPK = 3
CONV_WIDTH = 3
DEPTH = 1
ALPHA = (2.0 * DEPTH) ** 0.25
EPS = 1e-5
NEG = -1e30
LOG2E = 1.4426950408889634

LANES = 128
BF16_ROWS = 16
VMEM_LIMIT = 56 * 1024 * 1024

ROW_TILE = 512
INPROJ_ROWS = 1024
GLA_ROWS = 512
GLA_SUM_ROWS = 256
FF_CHUNK = 256
CONV_HALO = BF16_ROWS


def _dot(a, b):
    return jnp.dot(a, b, preferred_element_type=F32)


def _dot_nt(a, b):
    return lax.dot_general(a, b, (((1,), (1,)), ((), ())), preferred_element_type=F32)


def _dot_tn(a, b):
    return lax.dot_general(a, b, (((0,), (0,)), ((), ())), preferred_element_type=F32)


def _split_bf16(a):
    hi = a.astype(BF16)
    lo = (a - hi.astype(F32)).astype(BF16)
    return hi, lo


def _dot3(a, b):
    ah, al = _split_bf16(a)
    bh, bl = _split_bf16(b)
    return _dot(ah, bh) + (_dot(ah, bl) + _dot(al, bh))


def _layer_norm(x):
    mu = jnp.mean(x, -1, keepdims=True)
    xc = x - mu
    var = jnp.mean(xc * xc, -1, keepdims=True)
    return xc * lax.rsqrt(var + EPS)


def _params(*sem):
    return pltpu.CompilerParams(dimension_semantics=sem, vmem_limit_bytes=VMEM_LIMIT)


def _mod_kernel(c_ref, w_ref, b_ref, o_ref):
    o_ref[...] = _dot3(jax.nn.silu(c_ref[...]), w_ref[...]) + b_ref[...]


def _adaln_mod(c, w, b):
    bsz, d = c.shape
    n = w.shape[1]
    return pl.pallas_call(
        _mod_kernel,
        grid=(n // d,),
        in_specs=[
            pl.BlockSpec((bsz, d), lambda j: (0, 0)),
            pl.BlockSpec((d, d), lambda j: (0, j)),
            pl.BlockSpec((1, d), lambda j: (0, j)),
        ],
        out_specs=pl.BlockSpec((bsz, d), lambda j: (0, j)),
        out_shape=jax.ShapeDtypeStruct((bsz, n), F32),
        compiler_params=_params("parallel"),
        name="adaln_mod",
    )(c, w, b.reshape(1, n))


def _inproj_kernel(x_ref, mod_ref, wa_ref, wg_ref, wr_ref, wgate_ref, bgate_ref,
                   qa_ref, ka_ref, va_ref, qg_ref, kg_ref, vg_ref, gg_ref, la_ref):
    mod = mod_ref[...]
    aw, kw, vw = ATTN_WIDTH, GLA_K_WIDTH, GLA_V_WIDTH
    sub = min(x_ref.shape[0], ROW_TILE)
    tiles = [pl.ds(r0, sub) for r0 in range(0, x_ref.shape[0], sub)]
    us = [(_layer_norm(x_ref[at, :]) * (1.0 + mod[1:2]) + mod[0:1]).astype(BF16) for at in tiles]
    for at, u in zip(tiles, us):
        pa = _dot(u, wa_ref[...])
        qa_ref[at, :] = (pa[:, :aw] * (HEAD_DIM ** -0.5 * LOG2E)).astype(BF16)
        ka_ref[at, :] = pa[:, aw:2 * aw].astype(BF16)
        va_ref[at, :] = pa[:, 2 * aw:].astype(BF16)
        pg = _dot(u, wg_ref[...])
        qg_ref[at, :] = pg[:, :kw] * GLA_DK ** -0.5
        kg_ref[at, :] = pg[:, kw:2 * kw]
        vg_ref[at, :] = pg[:, 2 * kw:2 * kw + vw].astype(BF16)
        gg_ref[at, :] = pg[:, 2 * kw + vw:].astype(BF16)
        z = _dot3(_dot(u, wr_ref[...]), wgate_ref[...]) + bgate_ref[...]
        la_ref[at, :] = (jnp.minimum(z, 0.0) - jnp.log1p(jnp.exp(-jnp.abs(z)))) * (1.0 / GLA_TAU)


def _in_proj(x, mod, w_a, w_g, w_r, w_gate, b_gate):
    bsz, s, d = x.shape
    tm = min(INPROJ_ROWS, s)
    assert s % tm == 0 and tm % min(tm, ROW_TILE) == 0
    row = lambda b, i: (b, i, 0)
    const = lambda b, i: (0, 0)
    out = lambda w, dt: (pl.BlockSpec((None, tm, w), row), jax.ShapeDtypeStruct((bsz, s, w), dt))
    outs = [out(ATTN_WIDTH, BF16)] * 3 + [out(GLA_K_WIDTH, F32)] * 2 + [out(GLA_V_WIDTH, BF16)] * 2 \
        + [out(GLA_K_WIDTH, F32)]
    return pl.pallas_call(
        _inproj_kernel,
        grid=(bsz, s // tm),
        in_specs=[
            pl.BlockSpec((None, tm, d), row),
            pl.BlockSpec((None, 6, d), lambda b, i: (b, 0, 0)),
            pl.BlockSpec(w_a.shape, const),
            pl.BlockSpec(w_g.shape, const),
            pl.BlockSpec(w_r.shape, const),
            pl.BlockSpec(w_gate.shape, const),
            pl.BlockSpec(b_gate.shape, const),
        ],
        out_specs=[o[0] for o in outs],
        out_shape=[o[1] for o in outs],
        compiler_params=_params("parallel", "parallel"),
        name="in_proj",
    )(x, mod, w_a, w_g, w_r, w_gate, b_gate)


def _fold_lanes(x, op):
    parts = [x[:, t * LANES:(t + 1) * LANES] for t in range(x.shape[1] // LANES)]
    while len(parts) > 1:
        parts = [op(parts[t], parts[t + 1]) for t in range(0, len(parts), 2)]
    return parts[0]


def _moba_kernel(q_ref, qnext_ref, k_ref, v_ref, g_ref, o_ref, km_ref, kid_ref, vone_ref, qown_ref,
                 qpast_ref, s_ref, sda_ref, sdb_ref, mrun_ref, acc_ref, *, nb):
    blk = MOBA_BLOCK
    tq = 2 * blk
    half = LANES // 2
    nbp = km_ref.shape[0]
    t = pl.program_id(2)
    head0 = lax.broadcasted_iota(jnp.int32, (tq, LANES), 1) < half

    def prepare_queries(tile_ref, first_blk):
        q2 = tile_ref[...].astype(F32)
        zero = jnp.zeros_like(q2)
        q_own = (jnp.where(head0, q2, zero).astype(BF16), jnp.where(head0, zero, q2).astype(BF16))
        km_hi, km_lo = _split_bf16(km_ref[...])
        blk_id = lax.broadcasted_iota(jnp.int32, (nbp, tq), 0).astype(F32)
        second = jnp.where(lax.broadcasted_iota(jnp.int32, (nbp, tq), 1) >= blk, 1.0, 0.0)
        past = blk_id < first_blk + second

        def choose(qm):
            cur = jnp.where(past, _dot_nt(km_hi, qm) + _dot_nt(km_lo, qm), NEG)
            sel = jnp.zeros_like(cur)
            for _ in range(MOBA_TOPK):
                mx = jnp.max(cur, axis=0, keepdims=True)
                first = jnp.min(jnp.where(cur == mx, blk_id, float(nbp)), axis=0, keepdims=True)
                hit = blk_id == first
                sel = jnp.where(hit & past, 1.0, sel)
                cur = jnp.where(hit, -jnp.inf, cur)
            return jnp.where(sel > 0.0, 0.0, NEG)

        fill = jnp.zeros((half - nbp, tq), F32)
        bias = jnp.concatenate([choose(q_own[1]), fill, choose(q_own[0]), fill], axis=0).T
        qown_ref[0] = q_own[0]
        qown_ref[1] = q_own[1]
        qpast_ref[0] = jnp.where(head0, q2, bias).astype(BF16)
        qpast_ref[1] = jnp.where(head0, bias, q2).astype(BF16)

    @pl.when(t == 0)
    def _():
        km_ref[...] = jnp.zeros_like(km_ref)
        lane_k = lax.broadcasted_iota(jnp.int32, (blk, LANES), 1)
        h0 = lane_k < half
        one = jnp.ones((blk, LANES), F32)
        zero_k = jnp.zeros((blk, LANES), F32)
        for j in range(nb):
            at = pl.ds(j * blk, blk)
            kb = k_ref[at, :].astype(F32)
            vb = v_ref[at, :].astype(F32)
            km_ref[pl.ds(j, 1), :] = jnp.sum(kb, axis=0, keepdims=True) * (1.0 / blk)
            kid_ref[0, at, :] = jnp.where(h0, kb, jnp.where(lane_k == half + j, one, zero_k)).astype(BF16)
            kid_ref[1, at, :] = jnp.where(h0, jnp.where(lane_k == j, one, zero_k), kb).astype(BF16)
            vone_ref[0, at, :] = jnp.where(h0, vb, one).astype(BF16)
            vone_ref[1, at, :] = jnp.where(h0, one, vb).astype(BF16)
        prepare_queries(q_ref, 0.0)

    q_own = (qown_ref[0], qown_ref[1])
    q_past = (qpast_ref[0], qpast_ref[1])

    rows = lax.broadcasted_iota(jnp.int32, (blk, blk), 0)
    cols = lax.broadcasted_iota(jnp.int32, (blk, blk), 1)
    causal = cols <= rows
    row_a = pl.ds(pl.multiple_of(t * tq, tq), blk)
    row_b = pl.ds(pl.multiple_of(t * tq, tq) + blk, blk)
    own = pl.ds(pl.multiple_of(t * tq, tq), tq)
    halves = (pl.ds(0, blk), pl.ds(blk, blk))
    k_a = k_ref[row_a, :]
    k_b = k_ref[row_b, :]
    for h in range(2):
        s_a = jnp.where(causal, _dot_nt(q_own[h][:blk], k_a), NEG)
        s_b = jnp.concatenate([_dot_nt(q_past[h][blk:], kid_ref[h, row_a, :]),
                               jnp.where(causal, _dot_nt(q_own[h][blk:], k_b), NEG)], axis=1)
        sda_ref[h] = s_a
        sdb_ref[h] = s_b
        mrun_ref[h, halves[0], :] = _fold_lanes(s_a, jnp.maximum)
        mrun_ref[h, halves[1], :] = _fold_lanes(s_b, jnp.maximum)

    def for_each_key_pair(body):
        def two(j2, carry):
            body(2 * j2)
            body(2 * j2 + 1)
            return carry

        lax.fori_loop(0, lax.shift_right_logical(t, 1), two, 0)

        @pl.when((t & 1) == 1)
        def _():
            body(t - 1)

    def score_pair(jj):
        at = pl.ds(pl.multiple_of(jj * tq, tq), tq)
        for h in range(2):
            s = _dot_nt(q_past[h], kid_ref[h, at, :])
            s_ref[h, jj] = s
            mrun_ref[h] = jnp.maximum(mrun_ref[h], _fold_lanes(s, jnp.maximum))

    for_each_key_pair(score_pair)

    prepare_queries(qnext_ref, (2 * t + 2).astype(F32))

    m = [[jnp.max(mrun_ref[h, r, :], axis=-1, keepdims=True) for r in halves] for h in range(2)]
    for h in range(2):
        p_a = jnp.exp2(sda_ref[h] - m[h][0])
        p_b = jnp.exp2(sdb_ref[h] - m[h][1])
        acc_ref[h, halves[0], :] = _dot(p_a.astype(BF16), vone_ref[h, row_a, :])
        acc_ref[h, halves[1], :] = _dot(p_b.astype(BF16), vone_ref[h, own, :])

    def pv_pair(jj):
        at = pl.ds(pl.multiple_of(jj * tq, tq), tq)
        for h in range(2):
            p = jnp.concatenate(
                [jnp.exp2(s_ref[h, jj, r, :] - m[h][e]) for e, r in enumerate(halves)], axis=0)
            acc_ref[h] += _dot(p.astype(BF16), vone_ref[h, at, :])

    for_each_key_pair(pv_pair)

    acc = (acc_ref[0], acc_ref[1])
    row_sum = pltpu.roll(jnp.where(head0, acc[1], acc[0]), half, 1)
    o = jnp.where(head0, acc[0], acc[1]) / row_sum
    same_head = ((lax.broadcasted_iota(jnp.int32, (LANES, LANES), 0) < half)
                 == (lax.broadcasted_iota(jnp.int32, (LANES, LANES), 1) < half))
    avg = jnp.where(same_head, 1.0 / HEAD_DIM, 0.0).astype(BF16)
    sq_hi, sq_lo = _split_bf16(o * o)
    ms = _dot(sq_hi, avg) + _dot(sq_lo, avg)
    o_ref[...] = (o * lax.rsqrt(ms + EPS) * g_ref[...]).astype(BF16)


def _moba(qa, ka, va, g_pairs):
    bsz, s, w = qa.shape
    blk = MOBA_BLOCK
    tq = 2 * blk
    assert s % tq == 0 and w % LANES == 0
    nb = s // blk
    nbp = -(-nb // 8) * 8
    assert nbp <= LANES // 2
    npairs = w // LANES
    nt = s // tq
    qspec = pl.BlockSpec((None, tq, LANES), lambda b, p, t: (b, t, p))
    qnext = pl.BlockSpec((None, tq, LANES), lambda b, p, t: (b, jnp.minimum(t + 1, nt - 1), p))
    kvspec = pl.BlockSpec((None, s, LANES), lambda b, p, t: (b, 0, p))
    return pl.pallas_call(
        functools.partial(_moba_kernel, nb=nb),
        grid=(bsz, npairs, nt),
        in_specs=[qspec, qnext, kvspec, kvspec,
                  pl.BlockSpec((None, 1, LANES), lambda b, p, t: (p, 0, 0))],
        out_specs=qspec,
        out_shape=jax.ShapeDtypeStruct((bsz, s, w), BF16),
        scratch_shapes=[
            pltpu.VMEM((nbp, LANES), F32),
            pltpu.VMEM((2, s, LANES), BF16),
            pltpu.VMEM((2, s, LANES), BF16),
            pltpu.VMEM((2, tq, LANES), BF16),
            pltpu.VMEM((2, tq, LANES), BF16),
            pltpu.VMEM((2, max(s // tq - 1, 1), tq, tq), F32),
            pltpu.VMEM((2, blk, blk), F32),
            pltpu.VMEM((2, blk, tq), F32),
            pltpu.VMEM((2, tq, LANES), F32),
            pltpu.VMEM((2, tq, LANES), F32),
        ],
        compiler_params=_params("parallel", "parallel", "arbitrary"),
        name="moba",
    )(qa, qa, ka, va, g_pairs)


def _gla_kernel(q_ref, k_ref, la_ref, v_ref, gg_ref, g_ref, o_ref, st_ref):
    c = GLA_CHUNK
    dv = GLA_DV
    half = LANES // 2
    tr = q_ref.shape[0]
    shift = c.bit_length() - 1

    @pl.when(pl.program_id(2) == 0)
    def _():
        st_ref[...] = jnp.zeros_like(st_ref)

    grp = min(tr, GLA_SUM_ROWS)
    rows = lax.broadcasted_iota(jnp.int32, (grp, grp), 0)
    cols = lax.broadcasted_iota(jnp.int32, (grp, grp), 1)
    same_chunk = lax.shift_right_logical(rows, shift) == lax.shift_right_logical(cols, shift)
    upto = jnp.where(same_chunk & (cols <= rows), 1.0, 0.0).astype(BF16)
    b = []
    for g0 in range(0, tr, grp):
        la_hi, la_lo = _split_bf16(la_ref[pl.ds(g0, grp), :])
        b.append(_dot(upto, la_hi) + _dot(upto, la_lo))
    b = jnp.concatenate(b, axis=0)
    b_end = jnp.broadcast_to(b.reshape(tr // c, c, LANES)[:, c - 1:c, :],
                             (tr // c, c, LANES)).reshape(tr, LANES)
    kk = k_ref[...]
    q_dec = q_ref[...] * jnp.exp(b)
    k_dec = (kk * jnp.exp(-b)).astype(BF16)
    k_end = (kk * jnp.exp(b_end - b)).astype(BF16)
    decay = jnp.exp(b_end)
    head0 = lax.broadcasted_iota(jnp.int32, (tr, LANES), 1) < half
    zero = jnp.zeros_like(q_dec)
    q_h0 = jnp.where(head0, q_dec, zero).astype(BF16)
    q_h1 = jnp.where(head0, zero, q_dec).astype(BF16)
    in_chunk_row = lax.broadcasted_iota(jnp.int32, (2 * c, c), 0) & (c - 1)
    causal = lax.broadcasted_iota(jnp.int32, (2 * c, c), 1) <= in_chunk_row

    st = st_ref[...]
    outs = []
    for n in range(tr // c):
        at = slice(n * c, (n + 1) * c)
        qs = jnp.concatenate([q_h0[at], q_h1[at]], axis=0)
        attn = jnp.where(causal, _dot_nt(qs, k_dec[at]), 0.0)
        v_c = v_ref[pl.ds(n * c, c), :]
        res = _dot(attn.astype(BF16), v_c) + _dot_nt(qs, st.astype(BF16))
        outs.append(jnp.concatenate([res[:c, :dv], res[c:, dv:]], axis=1))
        st = st * decay[n * c:n * c + 1, :] + _dot_tn(v_c, k_end[at])
    st_ref[...] = st
    o = jnp.concatenate(outs, axis=0)

    for h in range(2):
        head = slice(h * dv, (h + 1) * dv)
        o_h = o[:, head]
        ms = jnp.mean(o_h * o_h, axis=-1, keepdims=True)
        y = o_h * lax.rsqrt(ms + EPS) * g_ref[:, head] * jax.nn.silu(gg_ref[:, head].astype(F32))
        o_ref[:, head] = y.astype(BF16)


def _gla(qg, kg, la, vg, gg, g_pairs):
    bsz, s, kw = qg.shape
    vw = vg.shape[-1]
    tr = min(GLA_ROWS, s)
    assert s % tr == 0 and tr % GLA_CHUNK == 0 and GLA_CHUNK & (GLA_CHUNK - 1) == 0
    npairs = kw // LANES
    kspec = pl.BlockSpec((None, tr, LANES), lambda b, p, t: (b, t, p))
    vspec = pl.BlockSpec((None, tr, 2 * GLA_DV), lambda b, p, t: (b, t, p))
    return pl.pallas_call(
        _gla_kernel,
        grid=(bsz, npairs, s // tr),
        in_specs=[kspec, kspec, kspec, vspec, vspec,
                  pl.BlockSpec((None, 1, 2 * GLA_DV), lambda b, p, t: (p, 0, 0))],
        out_specs=vspec,
        out_shape=jax.ShapeDtypeStruct((bsz, s, vw), BF16),
        scratch_shapes=[pltpu.VMEM((2 * GLA_DV, LANES), F32)],
        compiler_params=_params("parallel", "parallel", "arbitrary"),
        name="gla",
    )(qg, kg, la, vg, gg, g_pairs)


def _mix_ffn_kernel(oa_ref, oah_ref, og_ref, ogh_ref, x_ref, xh_ref, mod_ref, woa_ref, wog_ref,
                    g1_ref, b1_ref, wup_ref, bup_ref, cw_ref, cb_ref, wd_ref, g2_ref, b2_ref,
                    o_ref, x1_ref, a_ref):
    tm = x_ref.shape[0]
    halo = xh_ref.shape[0]
    d_ff = wd_ref.shape[0]
    mod = mod_ref[...]
    oa = jnp.concatenate([oah_ref[...], oa_ref[...]], axis=0)
    og = jnp.concatenate([ogh_ref[...], og_ref[...]], axis=0)
    xc = jnp.concatenate([xh_ref[...], x_ref[...]], axis=0)
    y = _dot(oa, woa_ref[...]) + _dot(og, wog_ref[...])
    x1 = _layer_norm(ALPHA * xc + mod[2:3] * y) * g1_ref[...] + b1_ref[...]
    ucat = (_layer_norm(x1) * (1.0 + mod[4:5]) + mod[3:4]).astype(BF16)
    x1_ref[...] = x1[halo:]
    u = ucat[halo:]
    rowid = lax.broadcasted_iota(jnp.int32, (halo + tm, 1), 0)
    keep = (rowid >= halo) | (pl.program_id(1) > 0)
    for c0 in range(0, d_ff, FF_CHUNK):
        val = slice(c0, c0 + FF_CHUNK)
        gate = slice(d_ff + c0, d_ff + c0 + FF_CHUNK)
        hv = _dot(u, wup_ref[:, val]) + bup_ref[:, val]
        hg = jnp.where(keep, _dot(ucat, wup_ref[:, gate]) + bup_ref[:, gate], 0.0)
        conv = cb_ref[:, val]
        for tap in range(CONV_WIDTH):
            back = CONV_WIDTH - 1 - tap
            conv = conv + cw_ref[tap:tap + 1, val] * hg[halo - back:halo - back + tm]
        a_ref[:, val] = (0.5 * conv * (1.0 + lax.erf(conv * (0.5 ** 0.5))) * hv).astype(BF16)
    h2 = ALPHA * x1_ref[...] + mod[5:6] * _dot(a_ref[...], wd_ref[...])
    o_ref[...] = _layer_norm(h2) * g2_ref[...] + b2_ref[...]


def _mix_ffn(oa, og, x, mod, w_oa, w_og, ln1_g, ln1_b, w_up, b_up, conv_w, conv_b, w_d, ln2_g, ln2_b):
    bsz, s, d = x.shape
    d_ff = w_d.shape[0]
    tm = min(ROW_TILE, s)
    halo = CONV_HALO
    assert s % tm == 0 and tm % halo == 0 and halo >= CONV_WIDTH - 1 and d_ff % FF_CHUNK == 0
    row = lambda b, i: (b, i, 0)
    prev = lambda b, i: (b, jnp.maximum(i * (tm // halo) - 1, 0), 0)
    const = lambda b, i: (0, 0)
    tile = lambda a: [pl.BlockSpec((None, tm, a.shape[-1]), row),
                      pl.BlockSpec((None, halo, a.shape[-1]), prev)]
    consts = (w_oa, w_og, ln1_g, ln1_b, w_up, b_up, conv_w, conv_b, w_d, ln2_g, ln2_b)
    return pl.pallas_call(
        _mix_ffn_kernel,
        grid=(bsz, s // tm),
        in_specs=tile(oa) + tile(og) + tile(x) + [pl.BlockSpec((None, 6, d), lambda b, i: (b, 0, 0))]
        + [pl.BlockSpec(a.shape, const) for a in consts],
        out_specs=pl.BlockSpec((None, tm, d), row),
        out_shape=jax.ShapeDtypeStruct((bsz, s, d), F32),
        scratch_shapes=[pltpu.VMEM((tm, d), F32), pltpu.VMEM((tm, d_ff), BF16)],
        compiler_params=_params("parallel", "parallel"),
        name="mix_ffn",
    )(oa, oa, og, og, x, x, mod, *consts)


def _layer(x, mod, w_in, w_gla_gate, b_gla_gate, attn_norm_g, gla_norm_g, w_o, ln1_g, ln1_b,
           w_up, b_up, conv_w, conv_b, w_down, ln2_g, ln2_b):
    d = x.shape[-1]
    d_ff = w_down.shape[0]
    a3 = 3 * ATTN_WIDTH
    g_end = a3 + 2 * GLA_K_WIDTH + 2 * GLA_V_WIDTH
    w_a = w_in[:, :a3].astype(BF16)
    w_g = w_in[:, a3:g_end].astype(BF16)
    w_r = jnp.pad(w_in[:, g_end:], ((0, 0), (0, LANES - GLA_GATE_RANK))).astype(BF16)
    w_gate = jnp.pad(w_gla_gate, ((0, LANES - GLA_GATE_RANK), (0, 0)))
    qa, ka, va, qg, kg, vg, gg, la = _in_proj(x, mod, w_a, w_g, w_r, w_gate,
                                              b_gla_gate.reshape(1, -1))
    oa = _moba(qa, ka, va, attn_norm_g.reshape(-1, 1, LANES))
    og = _gla(qg, kg, la, vg, gg, gla_norm_g.reshape(-1, 1, 2 * GLA_DV))
    w_ob = w_o.astype(BF16)
    return _mix_ffn(oa, og, x, mod, w_ob[:ATTN_WIDTH], w_ob[ATTN_WIDTH:],
                    ln1_g.reshape(1, d), ln1_b.reshape(1, d), w_up.astype(BF16), b_up.reshape(1, -1),
                    conv_w, conv_b.reshape(1, -1), w_down.astype(BF16),
                    ln2_g.reshape(1, d), ln2_b.reshape(1, d))


def kernel(x, c, w_ada, b_ada, w_in, w_gla_gate, b_gla_gate, attn_norm_g, gla_norm_g, w_o,
           ln1_g, ln1_b, w_up, b_up, conv_w, conv_b, w_down, ln2_g, ln2_b):
    bsz, _, d = x.shape
    for l in range(w_in.shape[0]):
        mod = _adaln_mod(c, w_ada[l], b_ada[l]).reshape(bsz, 6, d)
        x = _layer(x, mod, w_in[l], w_gla_gate[l], b_gla_gate[l], attn_norm_g[l], gla_norm_g[l],
                   w_o[l], ln1_g[l], ln1_b[l], w_up[l], b_up[l], conv_w[l], conv_b[l], w_down[l],
                   ln2_g[l], ln2_b[l])
    return x
```

```python
import functools

import jax
import jax.numpy as jnp
from jax import lax
from jax.experimental import pallas as pl
from jax.experimental.pallas import tpu as pltpu

F32 = jnp.float32
BF16 = jnp.bfloat16

HEAD_DIM = 64
N_ATTN_HEADS = 8
ATTN_WIDTH = N_ATTN_HEADS * HEAD_DIM
N_GLA_HEADS = 4
GLA_DK = 64
GLA_DV = 128
GLA_K_WIDTH = N_GLA_HEADS * GLA_DK
GLA_V_WIDTH = N_GLA_HEADS * GLA_DV
GLA_GATE_RANK = 16
GLA_TAU = 16.0
GLA_CHUNK = 64
MOBA_BLOCK = 256
MOBA_TOPK = 3
CONV_WIDTH = 3
DEPTH = 1
ALPHA = (2.0 * DEPTH) ** 0.25
EPS = 1e-5
NEG = -1e30
LOG2E = 1.4426950408889634

LANES = 128
BF16_ROWS = 16
VMEM_LIMIT = 56 * 1024 * 1024

ROW_TILE = 512
INPROJ_ROWS = 1024
GLA_ROWS = 512
GLA_SUM_ROWS = 256
FF_CHUNK = 256
CONV_HALO = BF16_ROWS


def _dot(a, b):
    return jnp.dot(a, b, preferred_element_type=F32)


def _dot_nt(a, b):
    return lax.dot_general(a, b, (((1,), (1,)), ((), ())), preferred_element_type=F32)


def _dot_tn(a, b):
    return lax.dot_general(a, b, (((0,), (0,)), ((), ())), preferred_element_type=F32)


def _split_bf16(a):
    hi = a.astype(BF16)
    lo = (a - hi.astype(F32)).astype(BF16)
    return hi, lo


def _dot3(a, b):
    ah, al = _split_bf16(a)
    bh, bl = _split_bf16(b)
    return _dot(ah, bh) + (_dot(ah, bl) + _dot(al, bh))


def _layer_norm(x):
    mu = jnp.mean(x, -1, keepdims=True)
    xc = x - mu
    var = jnp.mean(xc * xc, -1, keepdims=True)
    return xc * lax.rsqrt(var + EPS)


def _params(*sem):
    return pltpu.CompilerParams(dimension_semantics=sem, vmem_limit_bytes=VMEM_LIMIT)


def _mod_kernel(c_ref, w_ref, b_ref, o_ref):
    o_ref[...] = _dot3(jax.nn.silu(c_ref[...]), w_ref[...]) + b_ref[...]


def _adaln_mod(c, w, b):
    bsz, d = c.shape
    n = w.shape[1]
    return pl.pallas_call(
        _mod_kernel,
        grid=(n // d,),
        in_specs=[
            pl.BlockSpec((bsz, d), lambda j: (0, 0)),
            pl.BlockSpec((d, d), lambda j: (0, j)),
            pl.BlockSpec((1, d), lambda j: (0, j)),
        ],
        out_specs=pl.BlockSpec((bsz, d), lambda j: (0, j)),
        out_shape=jax.ShapeDtypeStruct((bsz, n), F32),
        compiler_params=_params("parallel"),
        name="adaln_mod",
    )(c, w, b.reshape(1, n))


def _inproj_kernel(x_ref, mod_ref, wa_ref, wg_ref, wr_ref, wgate_ref, bgate_ref,
                   qa_ref, ka_ref, va_ref, qg_ref, kg_ref, vg_ref, gg_ref, la_ref):
    mod = mod_ref[...]
    aw, kw, vw = ATTN_WIDTH, GLA_K_WIDTH, GLA_V_WIDTH
    sub = min(x_ref.shape[0], ROW_TILE)
    tiles = [pl.ds(r0, sub) for r0 in range(0, x_ref.shape[0], sub)]
    us = [(_layer_norm(x_ref[at, :]) * (1.0 + mod[1:2]) + mod[0:1]).astype(BF16) for at in tiles]
    for at, u in zip(tiles, us):
        pa = _dot(u, wa_ref[...])
        qa_ref[at, :] = (pa[:, :aw] * (HEAD_DIM ** -0.5 * LOG2E)).astype(BF16)
        ka_ref[at, :] = pa[:, aw:2 * aw].astype(BF16)
        va_ref[at, :] = pa[:, 2 * aw:].astype(BF16)
        pg = _dot(u, wg_ref[...])
        qg_ref[at, :] = pg[:, :kw] * GLA_DK ** -0.5
        kg_ref[at, :] = pg[:, kw:2 * kw]
        vg_ref[at, :] = pg[:, 2 * kw:2 * kw + vw].astype(BF16)
        gg_ref[at, :] = pg[:, 2 * kw + vw:].astype(BF16)
        z = _dot3(_dot(u, wr_ref[...]), wgate_ref[...]) + bgate_ref[...]
        la_ref[at, :] = (jnp.minimum(z, 0.0) - jnp.log1p(jnp.exp(-jnp.abs(z)))) * (1.0 / GLA_TAU)


def _in_proj(x, mod, w_a, w_g, w_r, w_gate, b_gate):
    bsz, s, d = x.shape
    tm = min(INPROJ_ROWS, s)
    assert s % tm == 0 and tm % min(tm, ROW_TILE) == 0
    row = lambda b, i: (b, i, 0)
    const = lambda b, i: (0, 0)
    out = lambda w, dt: (pl.BlockSpec((None, tm, w), row), jax.ShapeDtypeStruct((bsz, s, w), dt))
    outs = [out(ATTN_WIDTH, BF16)] * 3 + [out(GLA_K_WIDTH, F32)] * 2 + [out(GLA_V_WIDTH, BF16)] * 2 \
        + [out(GLA_K_WIDTH, F32)]
    return pl.pallas_call(
        _inproj_kernel,
        grid=(bsz, s // tm),
        in_specs=[
            pl.BlockSpec((None, tm, d), row),
            pl.BlockSpec((None, 6, d), lambda b, i: (b, 0, 0)),
            pl.BlockSpec(w_a.shape, const),
            pl.BlockSpec(w_g.shape, const),
            pl.BlockSpec(w_r.shape, const),
            pl.BlockSpec(w_gate.shape, const),
            pl.BlockSpec(b_gate.shape, const),
        ],
        out_specs=[o[0] for o in outs],
        out_shape=[o[1] for o in outs],
        compiler_params=_params("parallel", "parallel"),
        name="in_proj",
    )(x, mod, w_a, w_g, w_r, w_gate, b_gate)


def _fold_lanes(x, op):
    parts = [x[:, t * LANES:(t + 1) * LANES] for t in range(x.shape[1] // LANES)]
    while len(parts) > 1:
        parts = [op(parts[t], parts[t + 1]) for t in range(0, len(parts), 2)]
    return parts[0]


def _moba_kernel(q_ref, k_ref, v_ref, g_ref, o_ref, km_ref, kid_ref, vone_ref, qown_ref, qpast_ref,
                 s_ref, sda_ref, sdb_ref, mrun_ref, acc_ref, *, nb):
    blk = MOBA_BLOCK
    tq = 2 * blk
    half = LANES // 2
    nbp = km_ref.shape[0]
    nt = q_ref.shape[0] // tq
    head0 = lax.broadcasted_iota(jnp.int32, (tq, LANES), 1) < half

    def prepare_queries(t):
        q2 = q_ref[pl.ds(pl.multiple_of(t * tq, tq), tq), :].astype(F32)
        zero = jnp.zeros_like(q2)
        q_own = (jnp.where(head0, q2, zero).astype(BF16), jnp.where(head0, zero, q2).astype(BF16))
        km_hi, km_lo = _split_bf16(km_ref[...])
        blk_id = lax.broadcasted_iota(jnp.int32, (nbp, tq), 0).astype(F32)
        second = jnp.where(lax.broadcasted_iota(jnp.int32, (nbp, tq), 1) >= blk, 1.0, 0.0)
        past = blk_id < (2 * t).astype(F32) + second

        def choose(qm):
            cur = jnp.where(past, _dot_nt(km_hi, qm) + _dot_nt(km_lo, qm), NEG)
            sel = jnp.zeros_like(cur)
            for _ in range(MOBA_TOPK):
                mx = jnp.max(cur, axis=0, keepdims=True)
                first = jnp.min(jnp.where(cur == mx, blk_id, float(nbp)), axis=0, keepdims=True)
                hit = blk_id == first
                sel = jnp.where(hit & past, 1.0, sel)
                cur = jnp.where(hit, -jnp.inf, cur)
            return jnp.where(sel > 0.0, 0.0, NEG)

        fill = jnp.zeros((half - nbp, tq), F32)
        bias = jnp.concatenate([choose(q_own[1]), fill, choose(q_own[0]), fill], axis=0).T
        qown_ref[0] = q_own[0]
        qown_ref[1] = q_own[1]
        qpast_ref[0] = jnp.where(head0, q2, bias).astype(BF16)
        qpast_ref[1] = jnp.where(head0, bias, q2).astype(BF16)

    km_ref[...] = jnp.zeros_like(km_ref)
    lane_k = lax.broadcasted_iota(jnp.int32, (blk, LANES), 1)
    h0 = lane_k < half
    one = jnp.ones((blk, LANES), F32)
    zero_k = jnp.zeros((blk, LANES), F32)
    for j in range(nb):
        at = pl.ds(j * blk, blk)
        kb = k_ref[at, :].astype(F32)
        vb = v_ref[at, :].astype(F32)
        km_ref[pl.ds(j, 1), :] = jnp.sum(kb, axis=0, keepdims=True) * (1.0 / blk)
        kid_ref[0, at, :] = jnp.where(h0, kb, jnp.where(lane_k == half + j, one, zero_k)).astype(BF16)
        kid_ref[1, at, :] = jnp.where(h0, jnp.where(lane_k == j, one, zero_k), kb).astype(BF16)
        vone_ref[0, at, :] = jnp.where(h0, vb, one).astype(BF16)
        vone_ref[1, at, :] = jnp.where(h0, one, vb).astype(BF16)
    prepare_queries(jnp.int32(0))

    rows = lax.broadcasted_iota(jnp.int32, (blk, blk), 0)
    cols = lax.broadcasted_iota(jnp.int32, (blk, blk), 1)
    causal = cols <= rows
    halves = (pl.ds(0, blk), pl.ds(blk, blk))
    same_head = ((lax.broadcasted_iota(jnp.int32, (LANES, LANES), 0) < half)
                 == (lax.broadcasted_iota(jnp.int32, (LANES, LANES), 1) < half))
    avg = jnp.where(same_head, 1.0 / HEAD_DIM, 0.0).astype(BF16)

    def query_tile(t, carry):
        q_own = (qown_ref[0], qown_ref[1])
        q_past = (qpast_ref[0], qpast_ref[1])

        row_a = pl.ds(pl.multiple_of(t * tq, tq), blk)
        row_b = pl.ds(pl.multiple_of(t * tq, tq) + blk, blk)
        own = pl.ds(pl.multiple_of(t * tq, tq), tq)
        k_a = k_ref[row_a, :]
        k_b = k_ref[row_b, :]
        for h in range(2):
            s_a = jnp.where(causal, _dot_nt(q_own[h][:blk], k_a), NEG)
            s_b = jnp.concatenate([_dot_nt(q_past[h][blk:], kid_ref[h, row_a, :]),
                                   jnp.where(causal, _dot_nt(q_own[h][blk:], k_b), NEG)], axis=1)
            sda_ref[h] = s_a
            sdb_ref[h] = s_b
            mrun_ref[h, halves[0], :] = _fold_lanes(s_a, jnp.maximum)
            mrun_ref[h, halves[1], :] = _fold_lanes(s_b, jnp.maximum)

        def for_each_key_pair(body):
            def two(j2, c):
                body(2 * j2)
                body(2 * j2 + 1)
                return c

            lax.fori_loop(0, lax.shift_right_logical(t, 1), two, 0)

            @pl.when((t & 1) == 1)
            def _():
                body(t - 1)

        def score_pair(jj):
            at = pl.ds(pl.multiple_of(jj * tq, tq), tq)
            for h in range(2):
                s = _dot_nt(q_past[h], kid_ref[h, at, :])
                s_ref[h, jj] = s
                mrun_ref[h] = jnp.maximum(mrun_ref[h], _fold_lanes(s, jnp.maximum))

        for_each_key_pair(score_pair)

        prepare_queries(jnp.minimum(t + 1, nt - 1))

        m = [[jnp.max(mrun_ref[h, r, :], axis=-1, keepdims=True) for r in halves] for h in range(2)]
        for h in range(2):
            p_a = jnp.exp2(sda_ref[h] - m[h][0])
            p_b = jnp.exp2(sdb_ref[h] - m[h][1])
            acc_ref[h, halves[0], :] = _dot(p_a.astype(BF16), vone_ref[h, row_a, :])
            acc_ref[h, halves[1], :] = _dot(p_b.astype(BF16), vone_ref[h, own, :])

        def pv_pair(jj):
            at = pl.ds(pl.multiple_of(jj * tq, tq), tq)
            for h in range(2):
                p = jnp.concatenate(
                    [jnp.exp2(s_ref[h, jj, r, :] - m[h][e]) for e, r in enumerate(halves)], axis=0)
                acc_ref[h] += _dot(p.astype(BF16), vone_ref[h, at, :])

        for_each_key_pair(pv_pair)

        acc = (acc_ref[0], acc_ref[1])
        row_sum = pltpu.roll(jnp.where(head0, acc[1], acc[0]), half, 1)
        o = jnp.where(head0, acc[0], acc[1]) / row_sum
        sq_hi, sq_lo = _split_bf16(o * o)
        ms = _dot(sq_hi, avg) + _dot(sq_lo, avg)
        o_ref[own, :] = (o * lax.rsqrt(ms + EPS) * g_ref[...]).astype(BF16)
        return carry

    lax.fori_loop(0, nt, query_tile, 0)


def _moba(qa, ka, va, g_pairs):
    bsz, s, w = qa.shape
    blk = MOBA_BLOCK
    tq = 2 * blk
    assert s % tq == 0 and w % LANES == 0
    nb = s // blk
    nbp = -(-nb // 8) * 8
    assert nbp <= LANES // 2
    npairs = w // LANES
    seq = pl.BlockSpec((None, s, LANES), lambda b, p: (b, 0, p))
    return pl.pallas_call(
        functools.partial(_moba_kernel, nb=nb),
        grid=(bsz, npairs),
        in_specs=[seq, seq, seq, pl.BlockSpec((None, 1, LANES), lambda b, p: (p, 0, 0))],
        out_specs=seq,
        out_shape=jax.ShapeDtypeStruct((bsz, s, w), BF16),
        scratch_shapes=[
            pltpu.VMEM((nbp, LANES), F32),
            pltpu.VMEM((2, s, LANES), BF16),
            pltpu.VMEM((2, s, LANES), BF16),
            pltpu.VMEM((2, tq, LANES), BF16),
            pltpu.VMEM((2, tq, LANES), BF16),
            pltpu.VMEM((2, max(s // tq - 1, 1), tq, tq), F32),
            pltpu.VMEM((2, blk, blk), F32),
            pltpu.VMEM((2, blk, tq), F32),
            pltpu.VMEM((2, tq, LANES), F32),
            pltpu.VMEM((2, tq, LANES), F32),
        ],
        compiler_params=_params("parallel", "parallel"),
        name="moba",
    )(qa, ka, va, g_pairs)


def _gla_kernel(q_ref, k_ref, la_ref, v_ref, gg_ref, g_ref, o_ref):
    c = GLA_CHUNK
    dv = GLA_DV
    half = LANES // 2
    tr = min(GLA_ROWS, q_ref.shape[0])
    shift = c.bit_length() - 1
    grp = min(tr, GLA_SUM_ROWS)
    rows = lax.broadcasted_iota(jnp.int32, (grp, grp), 0)
    cols = lax.broadcasted_iota(jnp.int32, (grp, grp), 1)
    same_chunk = lax.shift_right_logical(rows, shift) == lax.shift_right_logical(cols, shift)
    upto = jnp.where(same_chunk & (cols <= rows), 1.0, 0.0).astype(BF16)
    head0 = lax.broadcasted_iota(jnp.int32, (tr, LANES), 1) < half
    in_chunk_row = lax.broadcasted_iota(jnp.int32, (2 * c, c), 0) & (c - 1)
    causal = lax.broadcasted_iota(jnp.int32, (2 * c, c), 1) <= in_chunk_row

    def row_tile(t, st):
        r0 = pl.multiple_of(t * tr, tr)
        at_tile = pl.ds(r0, tr)
        b = []
        for g0 in range(0, tr, grp):
            la_hi, la_lo = _split_bf16(la_ref[pl.ds(r0 + g0, grp), :])
            b.append(_dot(upto, la_hi) + _dot(upto, la_lo))
        b = jnp.concatenate(b, axis=0)
        b_end = jnp.broadcast_to(b.reshape(tr // c, c, LANES)[:, c - 1:c, :],
                                 (tr // c, c, LANES)).reshape(tr, LANES)
        kk = k_ref[at_tile, :]
        q_dec = q_ref[at_tile, :] * jnp.exp(b)
        k_dec = (kk * jnp.exp(-b)).astype(BF16)
        k_end = (kk * jnp.exp(b_end - b)).astype(BF16)
        decay = jnp.exp(b_end)
        zero = jnp.zeros_like(q_dec)
        q_h0 = jnp.where(head0, q_dec, zero).astype(BF16)
        q_h1 = jnp.where(head0, zero, q_dec).astype(BF16)

        outs = []
        for n in range(tr // c):
            at = slice(n * c, (n + 1) * c)
            qs = jnp.concatenate([q_h0[at], q_h1[at]], axis=0)
            attn = jnp.where(causal, _dot_nt(qs, k_dec[at]), 0.0)
            v_c = v_ref[pl.ds(r0 + n * c, c), :]
            res = _dot(attn.astype(BF16), v_c) + _dot_nt(qs, st.astype(BF16))
            outs.append(jnp.concatenate([res[:c, :dv], res[c:, dv:]], axis=1))
            st = st * decay[n * c:n * c + 1, :] + _dot_tn(v_c, k_end[at])
        o = jnp.concatenate(outs, axis=0)

        for h in range(2):
            head = slice(h * dv, (h + 1) * dv)
            o_h = o[:, head]
            ms = jnp.mean(o_h * o_h, axis=-1, keepdims=True)
            gate = gg_ref[at_tile, head].astype(F32)
            y = o_h * lax.rsqrt(ms + EPS) * g_ref[:, head] * jax.nn.silu(gate)
            o_ref[at_tile, head] = y.astype(BF16)
        return st

    lax.fori_loop(0, q_ref.shape[0] // tr, row_tile, jnp.zeros((2 * dv, LANES), F32))


def _gla(qg, kg, la, vg, gg, g_pairs):
    bsz, s, kw = qg.shape
    vw = vg.shape[-1]
    tr = min(GLA_ROWS, s)
    assert s % tr == 0 and tr % min(tr, GLA_SUM_ROWS) == 0 and tr % GLA_CHUNK == 0
    assert GLA_CHUNK & (GLA_CHUNK - 1) == 0
    npairs = kw // LANES
    kspec = pl.BlockSpec((None, s, LANES), lambda b, p: (b, 0, p))
    vspec = pl.BlockSpec((None, s, 2 * GLA_DV), lambda b, p: (b, 0, p))
    return pl.pallas_call(
        _gla_kernel,
        grid=(bsz, npairs),
        in_specs=[kspec, kspec, kspec, vspec, vspec,
                  pl.BlockSpec((None, 1, 2 * GLA_DV), lambda b, p: (p, 0, 0))],
        out_specs=vspec,
        out_shape=jax.ShapeDtypeStruct((bsz, s, vw), BF16),
        compiler_params=_params("parallel", "parallel"),
        name="gla",
    )(qg, kg, la, vg, gg, g_pairs)


def _mix_ffn_kernel(oa_ref, oah_ref, og_ref, ogh_ref, x_ref, xh_ref, mod_ref, woa_ref, wog_ref,
                    g1_ref, b1_ref, wup_ref, bup_ref, cw_ref, cb_ref, wd_ref, g2_ref, b2_ref,
                    o_ref, x1_ref, a_ref):
    tm = x_ref.shape[0]
    halo = xh_ref.shape[0]
    d_ff = wd_ref.shape[0]
    mod = mod_ref[...]
    oa = jnp.concatenate([oah_ref[...], oa_ref[...]], axis=0)
    og = jnp.concatenate([ogh_ref[...], og_ref[...]], axis=0)
    xc = jnp.concatenate([xh_ref[...], x_ref[...]], axis=0)
    y = _dot(oa, woa_ref[...]) + _dot(og, wog_ref[...])
    x1 = _layer_norm(ALPHA * xc + mod[2:3] * y) * g1_ref[...] + b1_ref[...]
    ucat = (_layer_norm(x1) * (1.0 + mod[4:5]) + mod[3:4]).astype(BF16)
    x1_ref[...] = x1[halo:]
    u = ucat[halo:]
    rowid = lax.broadcasted_iota(jnp.int32, (halo + tm, 1), 0)
    keep = (rowid >= halo) | (pl.program_id(1) > 0)
    for c0 in range(0, d_ff, FF_CHUNK):
        val = slice(c0, c0 + FF_CHUNK)
        gate = slice(d_ff + c0, d_ff + c0 + FF_CHUNK)
        hv = _dot(u, wup_ref[:, val]) + bup_ref[:, val]
        hg = jnp.where(keep, _dot(ucat, wup_ref[:, gate]) + bup_ref[:, gate], 0.0)
        conv = cb_ref[:, val]
        for tap in range(CONV_WIDTH):
            back = CONV_WIDTH - 1 - tap
            conv = conv + cw_ref[tap:tap + 1, val] * hg[halo - back:halo - back + tm]
        a_ref[:, val] = (0.5 * conv * (1.0 + lax.erf(conv * (0.5 ** 0.5))) * hv).astype(BF16)
    h2 = ALPHA * x1_ref[...] + mod[5:6] * _dot(a_ref[...], wd_ref[...])
    o_ref[...] = _layer_norm(h2) * g2_ref[...] + b2_ref[...]


def _mix_ffn(oa, og, x, mod, w_oa, w_og, ln1_g, ln1_b, w_up, b_up, conv_w, conv_b, w_d, ln2_g, ln2_b):
    bsz, s, d = x.shape
    d_ff = w_d.shape[0]
    tm = min(ROW_TILE, s)
    halo = CONV_HALO
    assert s % tm == 0 and tm % halo == 0 and halo >= CONV_WIDTH - 1 and d_ff % FF_CHUNK == 0
    row = lambda b, i: (b, i, 0)
    prev = lambda b, i: (b, jnp.maximum(i * (tm // halo) - 1, 0), 0)
    const = lambda b, i: (0, 0)
    tile = lambda a: [pl.BlockSpec((None, tm, a.shape[-1]), row),
                      pl.BlockSpec((None, halo, a.shape[-1]), prev)]
    consts = (w_oa, w_og, ln1_g, ln1_b, w_up, b_up, conv_w, conv_b, w_d, ln2_g, ln2_b)
    return pl.pallas_call(
        _mix_ffn_kernel,
        grid=(bsz, s // tm),
        in_specs=tile(oa) + tile(og) + tile(x) + [pl.BlockSpec((None, 6, d), lambda b, i: (b, 0, 0))]
        + [pl.BlockSpec(a.shape, const) for a in consts],
        out_specs=pl.BlockSpec((None, tm, d), row),
        out_shape=jax.ShapeDtypeStruct((bsz, s, d), F32),
        scratch_shapes=[pltpu.VMEM((tm, d), F32), pltpu.VMEM((tm, d_ff), BF16)],
        compiler_params=_params("parallel", "parallel"),
        name="mix_ffn",
    )(oa, oa, og, og, x, x, mod, *consts)


def _layer(x, mod, w_in, w_gla_gate, b_gla_gate, attn_norm_g, gla_norm_g, w_o, ln1_g, ln1_b,
           w_up, b_up, conv_w, conv_b, w_down, ln2_g, ln2_b):
    d = x.shape[-1]
    d_ff = w_down.shape[0]
    a3 = 3 * ATTN_WIDTH
    g_end = a3 + 2 * GLA_K_WIDTH + 2 * GLA_V_WIDTH
    w_a = w_in[:, :a3].astype(BF16)
    w_g = w_in[:, a3:g_end].astype(BF16)
    w_r = jnp.pad(w_in[:, g_end:], ((0, 0), (0, LANES - GLA_GATE_RANK))).astype(BF16)
    w_gate = jnp.pad(w_gla_gate, ((0, LANES - GLA_GATE_RANK), (0, 0)))
    qa, ka, va, qg, kg, vg, gg, la = _in_proj(x, mod, w_a, w_g, w_r, w_gate,
                                              b_gla_gate.reshape(1, -1))
    oa = _moba(qa, ka, va, attn_norm_g.reshape(-1, 1, LANES))
    og = _gla(qg, kg, la, vg, gg, gla_norm_g.reshape(-1, 1, 2 * GLA_DV))
    w_ob = w_o.astype(BF16)
    return _mix_ffn(oa, og, x, mod, w_ob[:ATTN_WIDTH], w_ob[ATTN_WIDTH:],
                    ln1_g.reshape(1, d), ln1_b.reshape(1, d), w_up.astype(BF16), b_up.reshape(1, -1),
                    conv_w, conv_b.reshape(1, -1), w_down.astype(BF16),
                    ln2_g.reshape(1, d), ln2_b.reshape(1, d))


def kernel(x, c, w_ada, b_ada, w_in, w_gla_gate, b_gla_gate, attn_norm_g, gla_norm_g, w_o,
           ln1_g, ln1_b, w_up, b_up, conv_w, conv_b, w_down, ln2_g, ln2_b):
    bsz, _, d = x.shape
    for l in range(w_in.shape[0]):
        mod = _adaln_mod(c, w_ada[l], b_ada[l]).reshape(bsz, 6, d)
        x = _layer(x, mod, w_in[l], w_gla_gate[l], b_gla_gate[l], attn_norm_g[l], gla_norm_g[l],
                   w_o[l], ln1_g[l], ln1_b[l], w_up[l], b_up[l], conv_w[l], conv_b[l], w_down[l],
                   ln2_g[l], ln2_b[l])
    return x
```

```python
import functools

import jax
import jax.numpy as jnp
from jax import lax
from jax.experimental import pallas as pl
from jax.experimental.pallas import tpu as pltpu

F32 = jnp.float32
BF16 = jnp.bfloat16

HEAD_DIM = 64
N_ATTN_HEADS = 8
ATTN_WIDTH = N_ATTN_HEADS * HEAD_DIM
N_GLA_HEADS = 4
GLA_DK = 64
GLA_DV = 128
GLA_K_WIDTH = N_GLA_HEADS * GLA_DK
GLA_V_WIDTH = N_GLA_HEADS * GLA_DV
GLA_GATE_RANK = 16
GLA_TAU = 16.0
GLA_CHUNK = 64
MOBA_BLOCK = 256
MOBA_TOPK = 3
CONV_WIDTH = 3
DEPTH = 1
ALPHA = (2.0 * DEPTH) ** 0.25
EPS = 1e-5
NEG = -1e30
LOG2E = 1.4426950408889634

LANES = 128
BF16_ROWS = 16
VMEM_LIMIT = 56 * 1024 * 1024

ROW_TILE = 512
INPROJ_ROWS = 1024
GLA_ROWS = 1024
GLA_SUM_ROWS = 256
FF_CHUNK = 256
CONV_HALO = BF16_ROWS


def _dot(a, b):
    return jnp.dot(a, b, preferred_element_type=F32)


def _dot_nt(a, b):
    return lax.dot_general(a, b, (((1,), (1,)), ((), ())), preferred_element_type=F32)


def _dot_tn(a, b):
    return lax.dot_general(a, b, (((0,), (0,)), ((), ())), preferred_element_type=F32)


def _split_bf16(a):
    hi = a.astype(BF16)
    lo = (a - hi.astype(F32)).astype(BF16)
    return hi, lo


def _dot3(a, b):
    ah, al = _split_bf16(a)
    bh, bl = _split_bf16(b)
    return _dot(ah, bh) + (_dot(ah, bl) + _dot(al, bh))


def _layer_norm(x):
    mu = jnp.mean(x, -1, keepdims=True)
    xc = x - mu
    var = jnp.mean(xc * xc, -1, keepdims=True)
    return xc * lax.rsqrt(var + EPS)


def _params(*sem):
    return pltpu.CompilerParams(dimension_semantics=sem, vmem_limit_bytes=VMEM_LIMIT)


def _mod_kernel(c_ref, w_ref, b_ref, o_ref):
    o_ref[...] = _dot3(jax.nn.silu(c_ref[...]), w_ref[...]) + b_ref[...]


def _adaln_mod(c, w, b):
    bsz, d = c.shape
    n = w.shape[1]
    return pl.pallas_call(
        _mod_kernel,
        grid=(n // d,),
        in_specs=[
            pl.BlockSpec((bsz, d), lambda j: (0, 0)),
            pl.BlockSpec((d, d), lambda j: (0, j)),
            pl.BlockSpec((1, d), lambda j: (0, j)),
        ],
        out_specs=pl.BlockSpec((bsz, d), lambda j: (0, j)),
        out_shape=jax.ShapeDtypeStruct((bsz, n), F32),
        compiler_params=_params("parallel"),
        name="adaln_mod",
    )(c, w, b.reshape(1, n))


def _inproj_kernel(x_ref, mod_ref, wa_ref, wg_ref, wr_ref, wgate_ref, bgate_ref,
                   qa_ref, ka_ref, va_ref, qg_ref, kg_ref, vg_ref, gg_ref, la_ref):
    mod = mod_ref[...]
    aw, kw, vw = ATTN_WIDTH, GLA_K_WIDTH, GLA_V_WIDTH
    sub = min(x_ref.shape[0], ROW_TILE)
    tiles = [pl.ds(r0, sub) for r0 in range(0, x_ref.shape[0], sub)]
    us = [(_layer_norm(x_ref[at, :]) * (1.0 + mod[1:2]) + mod[0:1]).astype(BF16) for at in tiles]
    for at, u in zip(tiles, us):
        pa = _dot(u, wa_ref[...])
        qa_ref[at, :] = (pa[:, :aw] * (HEAD_DIM ** -0.5 * LOG2E)).astype(BF16)
        ka_ref[at, :] = pa[:, aw:2 * aw].astype(BF16)
        va_ref[at, :] = pa[:, 2 * aw:].astype(BF16)
        pg = _dot(u, wg_ref[...])
        qg_ref[at, :] = pg[:, :kw] * GLA_DK ** -0.5
        kg_ref[at, :] = pg[:, kw:2 * kw]
        vg_ref[at, :] = pg[:, 2 * kw:2 * kw + vw].astype(BF16)
        gg_ref[at, :] = pg[:, 2 * kw + vw:].astype(BF16)
        z = _dot3(_dot(u, wr_ref[...]), wgate_ref[...]) + bgate_ref[...]
        la_ref[at, :] = (jnp.minimum(z, 0.0) - jnp.log1p(jnp.exp(-jnp.abs(z)))) * (1.0 / GLA_TAU)


def _in_proj(x, mod, w_a, w_g, w_r, w_gate, b_gate):
    bsz, s, d = x.shape
    tm = min(INPROJ_ROWS, s)
    assert s % tm == 0 and tm % min(tm, ROW_TILE) == 0
    row = lambda b, i: (b, i, 0)
    const = lambda b, i: (0, 0)
    out = lambda w, dt: (pl.BlockSpec((None, tm, w), row), jax.ShapeDtypeStruct((bsz, s, w), dt))
    outs = [out(ATTN_WIDTH, BF16)] * 3 + [out(GLA_K_WIDTH, F32)] * 2 + [out(GLA_V_WIDTH, BF16)] * 2 \
        + [out(GLA_K_WIDTH, F32)]
    return pl.pallas_call(
        _inproj_kernel,
        grid=(bsz, s // tm),
        in_specs=[
            pl.BlockSpec((None, tm, d), row),
            pl.BlockSpec((None, 6, d), lambda b, i: (b, 0, 0)),
            pl.BlockSpec(w_a.shape, const),
            pl.BlockSpec(w_g.shape, const),
            pl.BlockSpec(w_r.shape, const),
            pl.BlockSpec(w_gate.shape, const),
            pl.BlockSpec(b_gate.shape, const),
        ],
        out_specs=[o[0] for o in outs],
        out_shape=[o[1] for o in outs],
        compiler_params=_params("parallel", "parallel"),
        name="in_proj",
    )(x, mod, w_a, w_g, w_r, w_gate, b_gate)


def _fold_lanes(x, op):
    parts = [x[:, t * LANES:(t + 1) * LANES] for t in range(x.shape[1] // LANES)]
    while len(parts) > 1:
        parts = [op(parts[t], parts[t + 1]) for t in range(0, len(parts), 2)]
    return parts[0]


def _moba_kernel(q_ref, k_ref, v_ref, g_ref, o_ref, km_ref, kid_ref, vone_ref, qown_ref, qpast_ref,
                 s_ref, sda_ref, sdb_ref, mrun_ref, acc_ref, *, nb):
    blk = MOBA_BLOCK
    tq = 2 * blk
    half = LANES // 2
    nbp = km_ref.shape[0]
    nt = q_ref.shape[0] // tq
    head0 = lax.broadcasted_iota(jnp.int32, (tq, LANES), 1) < half

    def prepare_queries(t):
        q2 = q_ref[pl.ds(pl.multiple_of(t * tq, tq), tq), :].astype(F32)
        zero = jnp.zeros_like(q2)
        q_own = (jnp.where(head0, q2, zero).astype(BF16), jnp.where(head0, zero, q2).astype(BF16))
        km_hi, km_lo = _split_bf16(km_ref[...])
        blk_id = lax.broadcasted_iota(jnp.int32, (nbp, tq), 0).astype(F32)
        second = jnp.where(lax.broadcasted_iota(jnp.int32, (nbp, tq), 1) >= blk, 1.0, 0.0)
        past = blk_id < (2 * t).astype(F32) + second

        def choose(qm):
            cur = jnp.where(past, _dot_nt(km_hi, qm) + _dot_nt(km_lo, qm), NEG)
            sel = jnp.zeros_like(cur)
            for _ in range(MOBA_TOPK):
                mx = jnp.max(cur, axis=0, keepdims=True)
                first = jnp.min(jnp.where(cur == mx, blk_id, float(nbp)), axis=0, keepdims=True)
                hit = blk_id == first
                sel = jnp.where(hit & past, 1.0, sel)
                cur = jnp.where(hit, -jnp.inf, cur)
            return jnp.where(sel > 0.0, 0.0, NEG)

        fill = jnp.zeros((half - nbp, tq), F32)
        bias = jnp.concatenate([choose(q_own[1]), fill, choose(q_own[0]), fill], axis=0).T
        qown_ref[0] = q_own[0]
        qown_ref[1] = q_own[1]
        qpast_ref[0] = jnp.where(head0, q2, bias).astype(BF16)
        qpast_ref[1] = jnp.where(head0, bias, q2).astype(BF16)

    km_ref[...] = jnp.zeros_like(km_ref)
    lane_k = lax.broadcasted_iota(jnp.int32, (blk, LANES), 1)
    h0 = lane_k < half
    one = jnp.ones((blk, LANES), F32)
    zero_k = jnp.zeros((blk, LANES), F32)
    for j in range(nb):
        at = pl.ds(j * blk, blk)
        kb = k_ref[at, :].astype(F32)
        vb = v_ref[at, :].astype(F32)
        km_ref[pl.ds(j, 1), :] = jnp.sum(kb, axis=0, keepdims=True) * (1.0 / blk)
        kid_ref[0, at, :] = jnp.where(h0, kb, jnp.where(lane_k == half + j, one, zero_k)).astype(BF16)
        kid_ref[1, at, :] = jnp.where(h0, jnp.where(lane_k == j, one, zero_k), kb).astype(BF16)
        vone_ref[0, at, :] = jnp.where(h0, vb, one).astype(BF16)
        vone_ref[1, at, :] = jnp.where(h0, one, vb).astype(BF16)
    prepare_queries(jnp.int32(0))

    rows = lax.broadcasted_iota(jnp.int32, (blk, blk), 0)
    cols = lax.broadcasted_iota(jnp.int32, (blk, blk), 1)
    causal = cols <= rows
    halves = (pl.ds(0, blk), pl.ds(blk, blk))
    same_head = ((lax.broadcasted_iota(jnp.int32, (LANES, LANES), 0) < half)
                 == (lax.broadcasted_iota(jnp.int32, (LANES, LANES), 1) < half))
    avg = jnp.where(same_head, 1.0 / HEAD_DIM, 0.0).astype(BF16)

    def finish_tile(t):
        acc = (acc_ref[0], acc_ref[1])
        row_sum = pltpu.roll(jnp.where(head0, acc[1], acc[0]), half, 1)
        o = jnp.where(head0, acc[0], acc[1]) / row_sum
        sq_hi, sq_lo = _split_bf16(o * o)
        ms = _dot(sq_hi, avg) + _dot(sq_lo, avg)
        o_ref[pl.ds(pl.multiple_of(t * tq, tq), tq), :] = (
            o * lax.rsqrt(ms + EPS) * g_ref[...]).astype(BF16)

    acc_ref[...] = jnp.ones_like(acc_ref)

    def query_tile(t, carry):
        finish_tile(jnp.maximum(t - 1, 0))
        q_own = (qown_ref[0], qown_ref[1])
        q_past = (qpast_ref[0], qpast_ref[1])

        row_a = pl.ds(pl.multiple_of(t * tq, tq), blk)
        row_b = pl.ds(pl.multiple_of(t * tq, tq) + blk, blk)
        own = pl.ds(pl.multiple_of(t * tq, tq), tq)
        k_a = k_ref[row_a, :]
        k_b = k_ref[row_b, :]
        for h in range(2):
            s_a = jnp.where(causal, _dot_nt(q_own[h][:blk], k_a), NEG)
            s_b = jnp.concatenate([_dot_nt(q_past[h][blk:], kid_ref[h, row_a, :]),
                                   jnp.where(causal, _dot_nt(q_own[h][blk:], k_b), NEG)], axis=1)
            sda_ref[h] = s_a
            sdb_ref[h] = s_b
            mrun_ref[h, halves[0], :] = _fold_lanes(s_a, jnp.maximum)
            mrun_ref[h, halves[1], :] = _fold_lanes(s_b, jnp.maximum)

        def for_each_key_pair(body):
            def four(j4, c):
                for i in range(4):
                    body(4 * j4 + i)
                return c

            lax.fori_loop(0, lax.shift_right_logical(t, 2), four, 0)
            done = t & ~3

            @pl.when((t & 2) == 2)
            def _():
                body(done)
                body(done + 1)

            @pl.when((t & 1) == 1)
            def _():
                body(t - 1)

        def score_pair(jj):
            at = pl.ds(pl.multiple_of(jj * tq, tq), tq)
            for h in range(2):
                s = _dot_nt(q_past[h], kid_ref[h, at, :])
                s_ref[h, jj] = s
                mrun_ref[h] = jnp.maximum(mrun_ref[h], _fold_lanes(s, jnp.maximum))

        for_each_key_pair(score_pair)

        prepare_queries(jnp.minimum(t + 1, nt - 1))

        m = [[jnp.max(mrun_ref[h, r, :], axis=-1, keepdims=True) for r in halves] for h in range(2)]
        for h in range(2):
            p_a = jnp.exp2(sda_ref[h] - m[h][0])
            p_b = jnp.exp2(sdb_ref[h] - m[h][1])
            acc_ref[h, halves[0], :] = _dot(p_a.astype(BF16), vone_ref[h, row_a, :])
            acc_ref[h, halves[1], :] = _dot(p_b.astype(BF16), vone_ref[h, own, :])

        def pv_pair(jj):
            at = pl.ds(pl.multiple_of(jj * tq, tq), tq)
            for h in range(2):
                p = jnp.concatenate(
                    [jnp.exp2(s_ref[h, jj, r, :] - m[h][e]) for e, r in enumerate(halves)], axis=0)
                acc_ref[h] += _dot(p.astype(BF16), vone_ref[h, at, :])

        for_each_key_pair(pv_pair)
        return carry

    lax.fori_loop(0, nt, query_tile, 0)
    finish_tile(nt - 1)


def _moba(qa, ka, va, g_pairs):
    bsz, s, w = qa.shape
    blk = MOBA_BLOCK
    tq = 2 * blk
    assert s % tq == 0 and w % LANES == 0
    nb = s // blk
    nbp = -(-nb // 8) * 8
    assert nbp <= LANES // 2
    npairs = w // LANES
    seq = pl.BlockSpec((None, s, LANES), lambda b, p: (b, 0, p))
    return pl.pallas_call(
        functools.partial(_moba_kernel, nb=nb),
        grid=(bsz, npairs),
        in_specs=[seq, seq, seq, pl.BlockSpec((None, 1, LANES), lambda b, p: (p, 0, 0))],
        out_specs=seq,
        out_shape=jax.ShapeDtypeStruct((bsz, s, w), BF16),
        scratch_shapes=[
            pltpu.VMEM((nbp, LANES), F32),
            pltpu.VMEM((2, s, LANES), BF16),
            pltpu.VMEM((2, s, LANES), BF16),
            pltpu.VMEM((2, tq, LANES), BF16),
            pltpu.VMEM((2, tq, LANES), BF16),
            pltpu.VMEM((2, max(s // tq - 1, 1), tq, tq), F32),
            pltpu.VMEM((2, blk, blk), F32),
            pltpu.VMEM((2, blk, tq), F32),
            pltpu.VMEM((2, tq, LANES), F32),
            pltpu.VMEM((2, tq, LANES), F32),
        ],
        compiler_params=_params("parallel", "parallel"),
        name="moba",
    )(qa, ka, va, g_pairs)


def _gla_kernel(q_ref, k_ref, la_ref, v_ref, gg_ref, g_ref, o_ref):
    c = GLA_CHUNK
    dv = GLA_DV
    half = LANES // 2
    tr = min(GLA_ROWS, q_ref.shape[0])
    shift = c.bit_length() - 1
    grp = min(tr, GLA_SUM_ROWS)
    rows = lax.broadcasted_iota(jnp.int32, (grp, grp), 0)
    cols = lax.broadcasted_iota(jnp.int32, (grp, grp), 1)
    same_chunk = lax.shift_right_logical(rows, shift) == lax.shift_right_logical(cols, shift)
    upto = jnp.where(same_chunk & (cols <= rows), 1.0, 0.0).astype(BF16)
    head0 = lax.broadcasted_iota(jnp.int32, (tr, LANES), 1) < half
    in_chunk_row = lax.broadcasted_iota(jnp.int32, (2 * c, c), 0) & (c - 1)
    causal = lax.broadcasted_iota(jnp.int32, (2 * c, c), 1) <= in_chunk_row

    def row_tile(t, st):
        r0 = pl.multiple_of(t * tr, tr)
        at_tile = pl.ds(r0, tr)
        b = []
        for g0 in range(0, tr, grp):
            la_hi, la_lo = _split_bf16(la_ref[pl.ds(r0 + g0, grp), :])
            b.append(_dot(upto, la_hi) + _dot(upto, la_lo))
        b = jnp.concatenate(b, axis=0)
        b_end = jnp.broadcast_to(b.reshape(tr // c, c, LANES)[:, c - 1:c, :],
                                 (tr // c, c, LANES)).reshape(tr, LANES)
        kk = k_ref[at_tile, :]
        q_dec = q_ref[at_tile, :] * jnp.exp(b)
        k_dec = (kk * jnp.exp(-b)).astype(BF16)
        k_end = (kk * jnp.exp(b_end - b)).astype(BF16)
        decay = jnp.exp(b_end)
        zero = jnp.zeros_like(q_dec)
        q_h0 = jnp.where(head0, q_dec, zero).astype(BF16)
        q_h1 = jnp.where(head0, zero, q_dec).astype(BF16)

        outs = []
        for n in range(tr // c):
            at = slice(n * c, (n + 1) * c)
            qs = jnp.concatenate([q_h0[at], q_h1[at]], axis=0)
            attn = jnp.where(causal, _dot_nt(qs, k_dec[at]), 0.0)
            v_c = v_ref[pl.ds(r0 + n * c, c), :]
            res = _dot(attn.astype(BF16), v_c) + _dot_nt(qs, st.astype(BF16))
            outs.append(jnp.concatenate([res[:c, :dv], res[c:, dv:]], axis=1))
            st = st * decay[n * c:n * c + 1, :] + _dot_tn(v_c, k_end[at])
        o = jnp.concatenate(outs, axis=0)

        for h in range(2):
            head = slice(h * dv, (h + 1) * dv)
            o_h = o[:, head]
            ms = jnp.mean(o_h * o_h, axis=-1, keepdims=True)
            gate = gg_ref[at_tile, head].astype(F32)
            y = o_h * lax.rsqrt(ms + EPS) * g_ref[:, head] * jax.nn.silu(gate)
            o_ref[at_tile, head] = y.astype(BF16)
        return st

    lax.fori_loop(0, q_ref.shape[0] // tr, row_tile, jnp.zeros((2 * dv, LANES), F32))


def _gla(qg, kg, la, vg, gg, g_pairs):
    bsz, s, kw = qg.shape
    vw = vg.shape[-1]
    tr = min(GLA_ROWS, s)
    assert s % tr == 0 and tr % min(tr, GLA_SUM_ROWS) == 0 and tr % GLA_CHUNK == 0
    assert GLA_CHUNK & (GLA_CHUNK - 1) == 0
    npairs = kw // LANES
    kspec = pl.BlockSpec((None, s, LANES), lambda b, p: (b, 0, p))
    vspec = pl.BlockSpec((None, s, 2 * GLA_DV), lambda b, p: (b, 0, p))
    return pl.pallas_call(
        _gla_kernel,
        grid=(bsz, npairs),
        in_specs=[kspec, kspec, kspec, vspec, vspec,
                  pl.BlockSpec((None, 1, 2 * GLA_DV), lambda b, p: (p, 0, 0))],
        out_specs=vspec,
        out_shape=jax.ShapeDtypeStruct((bsz, s, vw), BF16),
        compiler_params=_params("parallel", "parallel"),
        name="gla",
    )(qg, kg, la, vg, gg, g_pairs)


def _mix_ffn_kernel(oa_ref, oah_ref, og_ref, ogh_ref, x_ref, xh_ref, mod_ref, woa_ref, wog_ref,
                    g1_ref, b1_ref, wup_ref, bup_ref, cw_ref, cb_ref, wd_ref, g2_ref, b2_ref,
                    o_ref, x1_ref, a_ref):
    tm = x_ref.shape[0]
    halo = xh_ref.shape[0]
    d_ff = wd_ref.shape[0]
    mod = mod_ref[...]
    oa = jnp.concatenate([oah_ref[...], oa_ref[...]], axis=0)
    og = jnp.concatenate([ogh_ref[...], og_ref[...]], axis=0)
    xc = jnp.concatenate([xh_ref[...], x_ref[...]], axis=0)
    y = _dot(oa, woa_ref[...]) + _dot(og, wog_ref[...])
    x1 = _layer_norm(ALPHA * xc + mod[2:3] * y) * g1_ref[...] + b1_ref[...]
    ucat = (_layer_norm(x1) * (1.0 + mod[4:5]) + mod[3:4]).astype(BF16)
    x1_ref[...] = x1[halo:]
    u = ucat[halo:]
    rowid = lax.broadcasted_iota(jnp.int32, (halo + tm, 1), 0)
    keep = (rowid >= halo) | (pl.program_id(1) > 0)
    for c0 in range(0, d_ff, FF_CHUNK):
        val = slice(c0, c0 + FF_CHUNK)
        gate = slice(d_ff + c0, d_ff + c0 + FF_CHUNK)
        hv = _dot(u, wup_ref[:, val]) + bup_ref[:, val]
        hg = jnp.where(keep, _dot(ucat, wup_ref[:, gate]) + bup_ref[:, gate], 0.0)
        conv = cb_ref[:, val]
        for tap in range(CONV_WIDTH):
            back = CONV_WIDTH - 1 - tap
            conv = conv + cw_ref[tap:tap + 1, val] * hg[halo - back:halo - back + tm]
        a_ref[:, val] = (0.5 * conv * (1.0 + lax.erf(conv * (0.5 ** 0.5))) * hv).astype(BF16)
    h2 = ALPHA * x1_ref[...] + mod[5:6] * _dot(a_ref[...], wd_ref[...])
    o_ref[...] = _layer_norm(h2) * g2_ref[...] + b2_ref[...]


def _mix_ffn(oa, og, x, mod, w_oa, w_og, ln1_g, ln1_b, w_up, b_up, conv_w, conv_b, w_d, ln2_g, ln2_b):
    bsz, s, d = x.shape
    d_ff = w_d.shape[0]
    tm = min(ROW_TILE, s)
    halo = CONV_HALO
    assert s % tm == 0 and tm % halo == 0 and halo >= CONV_WIDTH - 1 and d_ff % FF_CHUNK == 0
    row = lambda b, i: (b, i, 0)
    prev = lambda b, i: (b, jnp.maximum(i * (tm // halo) - 1, 0), 0)
    const = lambda b, i: (0, 0)
    tile = lambda a: [pl.BlockSpec((None, tm, a.shape[-1]), row),
                      pl.BlockSpec((None, halo, a.shape[-1]), prev)]
    consts = (w_oa, w_og, ln1_g, ln1_b, w_up, b_up, conv_w, conv_b, w_d, ln2_g, ln2_b)
    return pl.pallas_call(
        _mix_ffn_kernel,
        grid=(bsz, s // tm),
        in_specs=tile(oa) + tile(og) + tile(x) + [pl.BlockSpec((None, 6, d), lambda b, i: (b, 0, 0))]
        + [pl.BlockSpec(a.shape, const) for a in consts],
        out_specs=pl.BlockSpec((None, tm, d), row),
        out_shape=jax.ShapeDtypeStruct((bsz, s, d), F32),
        scratch_shapes=[pltpu.VMEM((tm, d), F32), pltpu.VMEM((tm, d_ff), BF16)],
        compiler_params=_params("parallel", "parallel"),
        name="mix_ffn",
    )(oa, oa, og, og, x, x, mod, *consts)


def _layer(x, mod, w_in, w_gla_gate, b_gla_gate, attn_norm_g, gla_norm_g, w_o, ln1_g, ln1_b,
           w_up, b_up, conv_w, conv_b, w_down, ln2_g, ln2_b):
    d = x.shape[-1]
    d_ff = w_down.shape[0]
    a3 = 3 * ATTN_WIDTH
    g_end = a3 + 2 * GLA_K_WIDTH + 2 * GLA_V_WIDTH
    w_a = w_in[:, :a3].astype(BF16)
    w_g = w_in[:, a3:g_end].astype(BF16)
    w_r = jnp.pad(w_in[:, g_end:], ((0, 0), (0, LANES - GLA_GATE_RANK))).astype(BF16)
    w_gate = jnp.pad(w_gla_gate, ((0, LANES - GLA_GATE_RANK), (0, 0)))
    qa, ka, va, qg, kg, vg, gg, la = _in_proj(x, mod, w_a, w_g, w_r, w_gate,
                                              b_gla_gate.reshape(1, -1))
    oa = _moba(qa, ka, va, attn_norm_g.reshape(-1, 1, LANES))
    og = _gla(qg, kg, la, vg, gg, gla_norm_g.reshape(-1, 1, 2 * GLA_DV))
    w_ob = w_o.astype(BF16)
    return _mix_ffn(oa, og, x, mod, w_ob[:ATTN_WIDTH], w_ob[ATTN_WIDTH:],
                    ln1_g.reshape(1, d), ln1_b.reshape(1, d), w_up.astype(BF16), b_up.reshape(1, -1),
                    conv_w, conv_b.reshape(1, -1), w_down.astype(BF16),
                    ln2_g.reshape(1, d), ln2_b.reshape(1, d))


def kernel(x, c, w_ada, b_ada, w_in, w_gla_gate, b_gla_gate, attn_norm_g, gla_norm_g, w_o,
           ln1_g, ln1_b, w_up, b_up, conv_w, conv_b, w_down, ln2_g, ln2_b):
    bsz, _, d = x.shape
    for l in range(w_in.shape[0]):
        mod = _adaln_mod(c, w_ada[l], b_ada[l]).reshape(bsz, 6, d)
        x = _layer(x, mod, w_in[l], w_gla_gate[l], b_gla_gate[l], attn_norm_g[l], gla_norm_g[l],
                   w_o[l], ln1_g[l], ln1_b[l], w_up[l], b_up[l], conv_w[l], conv_b[l], w_down[l],
                   ln2_g[l], ln2_b[l])
    return x
```

```python
import functools

import jax
import jax.numpy as jnp
from jax import lax
from jax.experimental import pallas as pl
from jax.experimental.pallas import tpu as pltpu

F32 = jnp.float32
BF16 = jnp.bfloat16

HEAD_DIM = 64
N_ATTN_HEADS = 8
ATTN_WIDTH = N_ATTN_HEADS * HEAD_DIM
N_GLA_HEADS = 4
GLA_DK = 64
GLA_DV = 128
GLA_K_WIDTH = N_GLA_HEADS * GLA_DK
GLA_V_WIDTH = N_GLA_HEADS * GLA_DV
GLA_GATE_RANK = 16
GLA_TAU = 16.0
GLA_CHUNK = 64
MOBA_BLOCK = 256
MOBA_TOPK = 3
CONV_WIDTH = 3
DEPTH = 1
ALPHA = (2.0 * DEPTH) ** 0.25
EPS = 1e-5
NEG = -1e30
LOG2E = 1.4426950408889634

LANES = 128
BF16_ROWS = 16
VMEM_LIMIT = 56 * 1024 * 1024

ROW_TILE = 512
INPROJ_ROWS = 1024
GLA_ROWS = 1024
GLA_SUM_ROWS = 256
FF_CHUNK = 256
CONV_HALO = BF16_ROWS


def _dot(a, b):
    return jnp.dot(a, b, preferred_element_type=F32)


def _dot_nt(a, b):
    return lax.dot_general(a, b, (((1,), (1,)), ((), ())), preferred_element_type=F32)


def _dot_tn(a, b):
    return lax.dot_general(a, b, (((0,), (0,)), ((), ())), preferred_element_type=F32)


def _split_bf16(a):
    hi = a.astype(BF16)
    lo = (a - hi.astype(F32)).astype(BF16)
    return hi, lo


def _dot3(a, b):
    ah, al = _split_bf16(a)
    bh, bl = _split_bf16(b)
    return _dot(ah, bh) + (_dot(ah, bl) + _dot(al, bh))


def _layer_norm(x):
    mu = jnp.mean(x, -1, keepdims=True)
    xc = x - mu
    var = jnp.mean(xc * xc, -1, keepdims=True)
    return xc * lax.rsqrt(var + EPS)


def _params(*sem):
    return pltpu.CompilerParams(dimension_semantics=sem, vmem_limit_bytes=VMEM_LIMIT)


def _mod_kernel(c_ref, w_ref, b_ref, o_ref):
    o_ref[...] = _dot3(jax.nn.silu(c_ref[...]), w_ref[...]) + b_ref[...]


def _adaln_mod(c, w, b):
    bsz, d = c.shape
    n = w.shape[1]
    return pl.pallas_call(
        _mod_kernel,
        grid=(n // d,),
        in_specs=[
            pl.BlockSpec((bsz, d), lambda j: (0, 0)),
            pl.BlockSpec((d, d), lambda j: (0, j)),
            pl.BlockSpec((1, d), lambda j: (0, j)),
        ],
        out_specs=pl.BlockSpec((bsz, d), lambda j: (0, j)),
        out_shape=jax.ShapeDtypeStruct((bsz, n), F32),
        compiler_params=_params("parallel"),
        name="adaln_mod",
    )(c, w, b.reshape(1, n))


def _inproj_kernel(x_ref, mod_ref, wa_ref, wg_ref, wr_ref, wgate_ref, bgate_ref,
                   qa_ref, ka_ref, va_ref, qg_ref, kg_ref, vg_ref, gg_ref, la_ref):
    mod = mod_ref[...]
    aw, kw, vw = ATTN_WIDTH, GLA_K_WIDTH, GLA_V_WIDTH
    sub = min(x_ref.shape[0], ROW_TILE)
    tiles = [pl.ds(r0, sub) for r0 in range(0, x_ref.shape[0], sub)]
    us = [(_layer_norm(x_ref[at, :]) * (1.0 + mod[1:2]) + mod[0:1]).astype(BF16) for at in tiles]
    for at, u in zip(tiles, us):
        pa = _dot(u, wa_ref[...])
        qa_ref[at, :] = (pa[:, :aw] * (HEAD_DIM ** -0.5 * LOG2E)).astype(BF16)
        ka_ref[at, :] = pa[:, aw:2 * aw].astype(BF16)
        va_ref[at, :] = pa[:, 2 * aw:].astype(BF16)
        pg = _dot(u, wg_ref[...])
        qg_ref[at, :] = pg[:, :kw] * GLA_DK ** -0.5
        kg_ref[at, :] = pg[:, kw:2 * kw]
        vg_ref[at, :] = pg[:, 2 * kw:2 * kw + vw].astype(BF16)
        gg_ref[at, :] = pg[:, 2 * kw + vw:].astype(BF16)
        z = _dot3(_dot(u, wr_ref[...]), wgate_ref[...]) + bgate_ref[...]
        la_ref[at, :] = (jnp.minimum(z, 0.0) - jnp.log1p(jnp.exp(-jnp.abs(z)))) * (1.0 / GLA_TAU)


def _in_proj(x, mod, w_a, w_g, w_r, w_gate, b_gate):
    bsz, s, d = x.shape
    tm = min(INPROJ_ROWS, s)
    assert s % tm == 0 and tm % min(tm, ROW_TILE) == 0
    row = lambda b, i: (b, i, 0)
    const = lambda b, i: (0, 0)
    out = lambda w, dt: (pl.BlockSpec((None, tm, w), row), jax.ShapeDtypeStruct((bsz, s, w), dt))
    outs = [out(ATTN_WIDTH, BF16)] * 3 + [out(GLA_K_WIDTH, F32)] * 2 + [out(GLA_V_WIDTH, BF16)] * 2 \
        + [out(GLA_K_WIDTH, F32)]
    return pl.pallas_call(
        _inproj_kernel,
        grid=(bsz, s // tm),
        in_specs=[
            pl.BlockSpec((None, tm, d), row),
            pl.BlockSpec((None, 6, d), lambda b, i: (b, 0, 0)),
            pl.BlockSpec(w_a.shape, const),
            pl.BlockSpec(w_g.shape, const),
            pl.BlockSpec(w_r.shape, const),
            pl.BlockSpec(w_gate.shape, const),
            pl.BlockSpec(b_gate.shape, const),
        ],
        out_specs=[o[0] for o in outs],
        out_shape=[o[1] for o in outs],
        compiler_params=_params("parallel", "parallel"),
        name="in_proj",
    )(x, mod, w_a, w_g, w_r, w_gate, b_gate)


def _fold_lanes(x, op):
    parts = [x[:, t * LANES:(t + 1) * LANES] for t in range(x.shape[1] // LANES)]
    while len(parts) > 1:
        parts = [op(parts[t], parts[t + 1]) for t in range(0, len(parts), 2)]
    return parts[0]


def _moba_kernel(q_ref, k_ref, v_ref, g_ref, o_ref, km_ref, kidt_ref, vone_ref, qown_ref, qpast_ref,
                 s_ref, sda_ref, sdb_ref, mrun_ref, acc_ref, *, nb):
    blk = MOBA_BLOCK
    tq = 2 * blk
    half = LANES // 2
    nbp = km_ref.shape[0]
    nt = q_ref.shape[0] // tq
    head0 = lax.broadcasted_iota(jnp.int32, (tq, LANES), 1) < half

    def prepare_queries(t):
        q2 = q_ref[pl.ds(pl.multiple_of(t * tq, tq), tq), :].astype(F32)
        zero = jnp.zeros_like(q2)
        q_own = (jnp.where(head0, q2, zero).astype(BF16), jnp.where(head0, zero, q2).astype(BF16))
        km_hi, km_lo = _split_bf16(km_ref[...])
        blk_id = lax.broadcasted_iota(jnp.int32, (nbp, tq), 0).astype(F32)
        second = jnp.where(lax.broadcasted_iota(jnp.int32, (nbp, tq), 1) >= blk, 1.0, 0.0)
        past = blk_id < (2 * t).astype(F32) + second

        def choose(qm):
            cur = jnp.where(past, _dot_nt(km_hi, qm) + _dot_nt(km_lo, qm), NEG)
            sel = jnp.zeros_like(cur)
            for _ in range(MOBA_TOPK):
                mx = jnp.max(cur, axis=0, keepdims=True)
                first = jnp.min(jnp.where(cur == mx, blk_id, float(nbp)), axis=0, keepdims=True)
                hit = blk_id == first
                sel = jnp.where(hit & past, 1.0, sel)
                cur = jnp.where(hit, -jnp.inf, cur)
            return jnp.where(sel > 0.0, 0.0, NEG)

        fill = jnp.zeros((half - nbp, tq), F32)
        bias = jnp.concatenate([choose(q_own[1]), fill, choose(q_own[0]), fill], axis=0).T
        qown_ref[0] = q_own[0]
        qown_ref[1] = q_own[1]
        qpast_ref[0] = jnp.where(head0, q2, bias).astype(BF16)
        qpast_ref[1] = jnp.where(head0, bias, q2).astype(BF16)

    km_ref[...] = jnp.zeros_like(km_ref)
    lane_k = lax.broadcasted_iota(jnp.int32, (blk, LANES), 1)
    h0 = lane_k < half
    one = jnp.ones((blk, LANES), F32)
    zero_k = jnp.zeros((blk, LANES), F32)
    for j in range(nb):
        at = pl.ds(j * blk, blk)
        kb = k_ref[at, :].astype(F32)
        vb = v_ref[at, :].astype(F32)
        km_ref[pl.ds(j, 1), :] = jnp.sum(kb, axis=0, keepdims=True) * (1.0 / blk)
        cols_j = pl.ds((j % 2) * blk, blk)
        kidt_ref[0, j // 2, :, cols_j] = jnp.where(
            h0, kb, jnp.where(lane_k == half + j, one, zero_k)).T.astype(BF16)
        kidt_ref[1, j // 2, :, cols_j] = jnp.where(
            h0, jnp.where(lane_k == j, one, zero_k), kb).T.astype(BF16)
        vone_ref[0, at, :] = jnp.where(h0, vb, one).astype(BF16)
        vone_ref[1, at, :] = jnp.where(h0, one, vb).astype(BF16)
    prepare_queries(jnp.int32(0))

    rows = lax.broadcasted_iota(jnp.int32, (blk, blk), 0)
    cols = lax.broadcasted_iota(jnp.int32, (blk, blk), 1)
    causal = cols <= rows
    halves = (pl.ds(0, blk), pl.ds(blk, blk))
    same_head = ((lax.broadcasted_iota(jnp.int32, (LANES, LANES), 0) < half)
                 == (lax.broadcasted_iota(jnp.int32, (LANES, LANES), 1) < half))
    avg = jnp.where(same_head, 1.0 / HEAD_DIM, 0.0).astype(BF16)

    def finish_tile(t):
        acc = (acc_ref[0], acc_ref[1])
        row_sum = pltpu.roll(jnp.where(head0, acc[1], acc[0]), half, 1)
        o = jnp.where(head0, acc[0], acc[1]) / row_sum
        sq_hi, sq_lo = _split_bf16(o * o)
        ms = _dot(sq_hi, avg) + _dot(sq_lo, avg)
        o_ref[pl.ds(pl.multiple_of(t * tq, tq), tq), :] = (
            o * lax.rsqrt(ms + EPS) * g_ref[...]).astype(BF16)

    acc_ref[...] = jnp.ones_like(acc_ref)

    def query_tile(t, carry):
        finish_tile(jnp.maximum(t - 1, 0))
        q_own = (qown_ref[0], qown_ref[1])
        q_past = (qpast_ref[0], qpast_ref[1])

        row_a = pl.ds(pl.multiple_of(t * tq, tq), blk)
        own = pl.ds(pl.multiple_of(t * tq, tq), tq)
        for h in range(2):
            kt = kidt_ref[h, t]
            s_a = jnp.where(causal, _dot(q_own[h][:blk], kt[:, :blk]), NEG)
            s_b = jnp.concatenate([_dot(q_past[h][blk:], kt[:, :blk]),
                                   jnp.where(causal, _dot(q_own[h][blk:], kt[:, blk:]), NEG)], axis=1)
            sda_ref[h] = s_a
            sdb_ref[h] = s_b
            mrun_ref[h, halves[0], :] = _fold_lanes(s_a, jnp.maximum)
            mrun_ref[h, halves[1], :] = _fold_lanes(s_b, jnp.maximum)

        def for_each_key_pair(body):
            def four(j4, c):
                for i in range(4):
                    body(4 * j4 + i)
                return c

            lax.fori_loop(0, lax.shift_right_logical(t, 2), four, 0)
            done = t & ~3

            @pl.when((t & 2) == 2)
            def _():
                body(done)
                body(done + 1)

            @pl.when((t & 1) == 1)
            def _():
                body(t - 1)

        def score_pair(jj):
            for h in range(2):
                s = _dot(q_past[h], kidt_ref[h, jj])
                s_ref[h, jj] = s
                mrun_ref[h] = jnp.maximum(mrun_ref[h], _fold_lanes(s, jnp.maximum))

        for_each_key_pair(score_pair)

        prepare_queries(jnp.minimum(t + 1, nt - 1))

        m = [[jnp.max(mrun_ref[h, r, :], axis=-1, keepdims=True) for r in halves] for h in range(2)]
        for h in range(2):
            p_a = jnp.exp2(sda_ref[h] - m[h][0])
            p_b = jnp.exp2(sdb_ref[h] - m[h][1])
            acc_ref[h, halves[0], :] = _dot(p_a.astype(BF16), vone_ref[h, row_a, :])
            acc_ref[h, halves[1], :] = _dot(p_b.astype(BF16), vone_ref[h, own, :])

        def pv_pair(jj):
            at = pl.ds(pl.multiple_of(jj * tq, tq), tq)
            for h in range(2):
                p = jnp.concatenate(
                    [jnp.exp2(s_ref[h, jj, r, :] - m[h][e]) for e, r in enumerate(halves)], axis=0)
                acc_ref[h] += _dot(p.astype(BF16), vone_ref[h, at, :])

        for_each_key_pair(pv_pair)
        return carry

    lax.fori_loop(0, nt, query_tile, 0)
    finish_tile(nt - 1)


def _moba(qa, ka, va, g_pairs):
    bsz, s, w = qa.shape
    blk = MOBA_BLOCK
    tq = 2 * blk
    assert s % tq == 0 and w % LANES == 0
    nb = s // blk
    nbp = -(-nb // 8) * 8
    assert nbp <= LANES // 2
    npairs = w // LANES
    seq = pl.BlockSpec((None, s, LANES), lambda b, p: (b, 0, p))
    return pl.pallas_call(
        functools.partial(_moba_kernel, nb=nb),
        grid=(bsz, npairs),
        in_specs=[seq, seq, seq, pl.BlockSpec((None, 1, LANES), lambda b, p: (p, 0, 0))],
        out_specs=seq,
        out_shape=jax.ShapeDtypeStruct((bsz, s, w), BF16),
        scratch_shapes=[
            pltpu.VMEM((nbp, LANES), F32),
            pltpu.VMEM((2, s // tq, LANES, tq), BF16),
            pltpu.VMEM((2, s, LANES), BF16),
            pltpu.VMEM((2, tq, LANES), BF16),
            pltpu.VMEM((2, tq, LANES), BF16),
            pltpu.VMEM((2, max(s // tq - 1, 1), tq, tq), F32),
            pltpu.VMEM((2, blk, blk), F32),
            pltpu.VMEM((2, blk, tq), F32),
            pltpu.VMEM((2, tq, LANES), F32),
            pltpu.VMEM((2, tq, LANES), F32),
        ],
        compiler_params=_params("parallel", "parallel"),
        name="moba",
    )(qa, ka, va, g_pairs)


def _gla_kernel(q_ref, k_ref, la_ref, v_ref, gg_ref, g_ref, o_ref):
    c = GLA_CHUNK
    dv = GLA_DV
    half = LANES // 2
    tr = min(GLA_ROWS, q_ref.shape[0])
    shift = c.bit_length() - 1
    grp = min(tr, GLA_SUM_ROWS)
    rows = lax.broadcasted_iota(jnp.int32, (grp, grp), 0)
    cols = lax.broadcasted_iota(jnp.int32, (grp, grp), 1)
    same_chunk = lax.shift_right_logical(rows, shift) == lax.shift_right_logical(cols, shift)
    upto = jnp.where(same_chunk & (cols <= rows), 1.0, 0.0).astype(BF16)
    head0 = lax.broadcasted_iota(jnp.int32, (tr, LANES), 1) < half
    in_chunk_row = lax.broadcasted_iota(jnp.int32, (2 * c, c), 0) & (c - 1)
    causal = lax.broadcasted_iota(jnp.int32, (2 * c, c), 1) <= in_chunk_row

    def row_tile(t, st):
        r0 = pl.multiple_of(t * tr, tr)
        at_tile = pl.ds(r0, tr)
        b = []
        for g0 in range(0, tr, grp):
            la_hi, la_lo = _split_bf16(la_ref[pl.ds(r0 + g0, grp), :])
            b.append(_dot(upto, la_hi) + _dot(upto, la_lo))
        b = jnp.concatenate(b, axis=0)
        b_end = jnp.broadcast_to(b.reshape(tr // c, c, LANES)[:, c - 1:c, :],
                                 (tr // c, c, LANES)).reshape(tr, LANES)
        kk = k_ref[at_tile, :]
        q_dec = q_ref[at_tile, :] * jnp.exp(b)
        k_dec = (kk * jnp.exp(-b)).astype(BF16)
        k_end = (kk * jnp.exp(b_end - b)).astype(BF16)
        decay = jnp.exp(b_end)
        zero = jnp.zeros_like(q_dec)
        q_h0 = jnp.where(head0, q_dec, zero).astype(BF16)
        q_h1 = jnp.where(head0, zero, q_dec).astype(BF16)

        outs = []
        for n in range(tr // c):
            at = slice(n * c, (n + 1) * c)
            qs = jnp.concatenate([q_h0[at], q_h1[at]], axis=0)
            attn = jnp.where(causal, _dot_nt(qs, k_dec[at]), 0.0)
            v_c = v_ref[pl.ds(r0 + n * c, c), :]
            res = _dot(attn.astype(BF16), v_c) + _dot_nt(qs, st.astype(BF16))
            outs.append(jnp.concatenate([res[:c, :dv], res[c:, dv:]], axis=1))
            st = st * decay[n * c:n * c + 1, :] + _dot_tn(v_c, k_end[at])
        o = jnp.concatenate(outs, axis=0)

        for h in range(2):
            head = slice(h * dv, (h + 1) * dv)
            o_h = o[:, head]
            ms = jnp.mean(o_h * o_h, axis=-1, keepdims=True)
            gate = gg_ref[at_tile, head].astype(F32)
            y = o_h * lax.rsqrt(ms + EPS) * g_ref[:, head] * jax.nn.silu(gate)
            o_ref[at_tile, head] = y.astype(BF16)
        return st

    lax.fori_loop(0, q_ref.shape[0] // tr, row_tile, jnp.zeros((2 * dv, LANES), F32))


def _gla(qg, kg, la, vg, gg, g_pairs):
    bsz, s, kw = qg.shape
    vw = vg.shape[-1]
    tr = min(GLA_ROWS, s)
    assert s % tr == 0 and tr % min(tr, GLA_SUM_ROWS) == 0 and tr % GLA_CHUNK == 0
    assert GLA_CHUNK & (GLA_CHUNK - 1) == 0
    npairs = kw // LANES
    kspec = pl.BlockSpec((None, s, LANES), lambda b, p: (b, 0, p))
    vspec = pl.BlockSpec((None, s, 2 * GLA_DV), lambda b, p: (b, 0, p))
    return pl.pallas_call(
        _gla_kernel,
        grid=(bsz, npairs),
        in_specs=[kspec, kspec, kspec, vspec, vspec,
                  pl.BlockSpec((None, 1, 2 * GLA_DV), lambda b, p: (p, 0, 0))],
        out_specs=vspec,
        out_shape=jax.ShapeDtypeStruct((bsz, s, vw), BF16),
        compiler_params=_params("parallel", "parallel"),
        name="gla",
    )(qg, kg, la, vg, gg, g_pairs)


def _mix_ffn_kernel(oa_ref, oah_ref, og_ref, ogh_ref, x_ref, xh_ref, mod_ref, woa_ref, wog_ref,
                    g1_ref, b1_ref, wup_ref, bup_ref, cw_ref, cb_ref, wd_ref, g2_ref, b2_ref,
                    o_ref, x1_ref, a_ref):
    tm = x_ref.shape[0]
    halo = xh_ref.shape[0]
    d_ff = wd_ref.shape[0]
    mod = mod_ref[...]
    oa = jnp.concatenate([oah_ref[...], oa_ref[...]], axis=0)
    og = jnp.concatenate([ogh_ref[...], og_ref[...]], axis=0)
    xc = jnp.concatenate([xh_ref[...], x_ref[...]], axis=0)
    y = _dot(oa, woa_ref[...]) + _dot(og, wog_ref[...])
    x1 = _layer_norm(ALPHA * xc + mod[2:3] * y) * g1_ref[...] + b1_ref[...]
    ucat = (_layer_norm(x1) * (1.0 + mod[4:5]) + mod[3:4]).astype(BF16)
    x1_ref[...] = x1[halo:]
    u = ucat[halo:]
    rowid = lax.broadcasted_iota(jnp.int32, (halo + tm, 1), 0)
    keep = (rowid >= halo) | (pl.program_id(1) > 0)
    for c0 in range(0, d_ff, FF_CHUNK):
        val = slice(c0, c0 + FF_CHUNK)
        gate = slice(d_ff + c0, d_ff + c0 + FF_CHUNK)
        hv = _dot(u, wup_ref[:, val]) + bup_ref[:, val]
        hg = jnp.where(keep, _dot(ucat, wup_ref[:, gate]) + bup_ref[:, gate], 0.0)
        conv = cb_ref[:, val]
        for tap in range(CONV_WIDTH):
            back = CONV_WIDTH - 1 - tap
            conv = conv + cw_ref[tap:tap + 1, val] * hg[halo - back:halo - back + tm]
        a_ref[:, val] = (0.5 * conv * (1.0 + lax.erf(conv * (0.5 ** 0.5))) * hv).astype(BF16)
    h2 = ALPHA * x1_ref[...] + mod[5:6] * _dot(a_ref[...], wd_ref[...])
    o_ref[...] = _layer_norm(h2) * g2_ref[...] + b2_ref[...]


def _mix_ffn(oa, og, x, mod, w_oa, w_og, ln1_g, ln1_b, w_up, b_up, conv_w, conv_b, w_d, ln2_g, ln2_b):
    bsz, s, d = x.shape
    d_ff = w_d.shape[0]
    tm = min(ROW_TILE, s)
    halo = CONV_HALO
    assert s % tm == 0 and tm % halo == 0 and halo >= CONV_WIDTH - 1 and d_ff % FF_CHUNK == 0
    row = lambda b, i: (b, i, 0)
    prev = lambda b, i: (b, jnp.maximum(i * (tm // halo) - 1, 0), 0)
    const = lambda b, i: (0, 0)
    tile = lambda a: [pl.BlockSpec((None, tm, a.shape[-1]), row),
                      pl.BlockSpec((None, halo, a.shape[-1]), prev)]
    consts = (w_oa, w_og, ln1_g, ln1_b, w_up, b_up, conv_w, conv_b, w_d, ln2_g, ln2_b)
    return pl.pallas_call(
        _mix_ffn_kernel,
        grid=(bsz, s // tm),
        in_specs=tile(oa) + tile(og) + tile(x) + [pl.BlockSpec((None, 6, d), lambda b, i: (b, 0, 0))]
        + [pl.BlockSpec(a.shape, const) for a in consts],
        out_specs=pl.BlockSpec((None, tm, d), row),
        out_shape=jax.ShapeDtypeStruct((bsz, s, d), F32),
        scratch_shapes=[pltpu.VMEM((tm, d), F32), pltpu.VMEM((tm, d_ff), BF16)],
        compiler_params=_params("parallel", "parallel"),
        name="mix_ffn",
    )(oa, oa, og, og, x, x, mod, *consts)


def _layer(x, mod, w_in, w_gla_gate, b_gla_gate, attn_norm_g, gla_norm_g, w_o, ln1_g, ln1_b,
           w_up, b_up, conv_w, conv_b, w_down, ln2_g, ln2_b):
    d = x.shape[-1]
    d_ff = w_down.shape[0]
    a3 = 3 * ATTN_WIDTH
    g_end = a3 + 2 * GLA_K_WIDTH + 2 * GLA_V_WIDTH
    w_a = w_in[:, :a3].astype(BF16)
    w_g = w_in[:, a3:g_end].astype(BF16)
    w_r = jnp.pad(w_in[:, g_end:], ((0, 0), (0, LANES - GLA_GATE_RANK))).astype(BF16)
    w_gate = jnp.pad(w_gla_gate, ((0, LANES - GLA_GATE_RANK), (0, 0)))
    qa, ka, va, qg, kg, vg, gg, la = _in_proj(x, mod, w_a, w_g, w_r, w_gate,
                                              b_gla_gate.reshape(1, -1))
    oa = _moba(qa, ka, va, attn_norm_g.reshape(-1, 1, LANES))
    og = _gla(qg, kg, la, vg, gg, gla_norm_g.reshape(-1, 1, 2 * GLA_DV))
    w_ob = w_o.astype(BF16)
    return _mix_ffn(oa, og, x, mod, w_ob[:ATTN_WIDTH], w_ob[ATTN_WIDTH:],
                    ln1_g.reshape(1, d), ln1_b.reshape(1, d), w_up.astype(BF16), b_up.reshape(1, -1),
                    conv_w, conv_b.reshape(1, -1), w_down.astype(BF16),
                    ln2_g.reshape(1, d), ln2_b.reshape(1, d))


def kernel(x, c, w_ada, b_ada, w_in, w_gla_gate, b_gla_gate, attn_norm_g, gla_norm_g, w_o,
           ln1_g, ln1_b, w_up, b_up, conv_w, conv_b, w_down, ln2_g, ln2_b):
    bsz, _, d = x.shape
    for l in range(w_in.shape[0]):
        mod = _adaln_mod(c, w_ada[l], b_ada[l]).reshape(bsz, 6, d)
        x = _layer(x, mod, w_in[l], w_gla_gate[l], b_gla_gate[l], attn_norm_g[l], gla_norm_g[l],
                   w_o[l], ln1_g[l], ln1_b[l], w_up[l], b_up[l], conv_w[l], conv_b[l], w_down[l],
                   ln2_g[l], ln2_b[l])
    return x
```

```python
import functools

import jax
import jax.numpy as jnp
from jax import lax
from jax.experimental import pallas as pl
from jax.experimental.pallas import tpu as pltpu

F32 = jnp.float32
BF16 = jnp.bfloat16

HEAD_DIM = 64
N_ATTN_HEADS = 8
ATTN_WIDTH = N_ATTN_HEADS * HEAD_DIM
N_GLA_HEADS = 4
GLA_DK = 64
GLA_DV = 128
GLA_K_WIDTH = N_GLA_HEADS * GLA_DK
GLA_V_WIDTH = N_GLA_HEADS * GLA_DV
GLA_GATE_RANK = 16
GLA_TAU = 16.0
GLA_CHUNK = 64
MOBA_BLOCK = 256
MOBA_TOPK = 3
CONV_WIDTH = 3
DEPTH = 1
ALPHA = (2.0 * DEPTH) ** 0.25
EPS = 1e-5
NEG = -1e30
LOG2E = 1.4426950408889634

LANES = 128
BF16_ROWS = 16
VMEM_LIMIT = 56 * 1024 * 1024

ROW_TILE = 512
FFN_ROWS = 1024
INPROJ_ROWS = 1024
GLA_ROWS = 2048
GLA_SUM_ROWS = 256
FF_CHUNK = 256
CONV_HALO = BF16_ROWS


def _dot(a, b):
    return jnp.dot(a, b, preferred_element_type=F32)


def _dot_nt(a, b):
    return lax.dot_general(a, b, (((1,), (1,)), ((), ())), preferred_element_type=F32)


def _dot_tn(a, b):
    return lax.dot_general(a, b, (((0,), (0,)), ((), ())), preferred_element_type=F32)


def _split_bf16(a):
    hi = a.astype(BF16)
    lo = (a - hi.astype(F32)).astype(BF16)
    return hi, lo


def _dot3(a, b):
    ah, al = _split_bf16(a)
    bh, bl = _split_bf16(b)
    return _dot(ah, bh) + (_dot(ah, bl) + _dot(al, bh))


def _layer_norm(x):
    mu = jnp.mean(x, -1, keepdims=True)
    xc = x - mu
    var = jnp.mean(xc * xc, -1, keepdims=True)
    return xc * lax.rsqrt(var + EPS)


def _params(*sem):
    return pltpu.CompilerParams(dimension_semantics=sem, vmem_limit_bytes=VMEM_LIMIT)


def _mod_kernel(c_ref, w_ref, b_ref, o_ref):
    o_ref[...] = _dot3(jax.nn.silu(c_ref[...]), w_ref[...]) + b_ref[...]


def _adaln_mod(c, w, b):
    bsz, d = c.shape
    n = w.shape[1]
    return pl.pallas_call(
        _mod_kernel,
        grid=(n // d,),
        in_specs=[
            pl.BlockSpec((bsz, d), lambda j: (0, 0)),
            pl.BlockSpec((d, d), lambda j: (0, j)),
            pl.BlockSpec((1, d), lambda j: (0, j)),
        ],
        out_specs=pl.BlockSpec((bsz, d), lambda j: (0, j)),
        out_shape=jax.ShapeDtypeStruct((bsz, n), F32),
        compiler_params=_params("parallel"),
        name="adaln_mod",
    )(c, w, b.reshape(1, n))


def _inproj_kernel(x_ref, mod_ref, wa_ref, wg_ref, wr_ref, wgate_ref, bgate_ref,
                   qa_ref, ka_ref, va_ref, qg_ref, kg_ref, vg_ref, gg_ref, la_ref):
    mod = mod_ref[...]
    aw, kw, vw = ATTN_WIDTH, GLA_K_WIDTH, GLA_V_WIDTH
    sub = min(x_ref.shape[0], ROW_TILE)
    tiles = [pl.ds(r0, sub) for r0 in range(0, x_ref.shape[0], sub)]
    us = [(_layer_norm(x_ref[at, :]) * (1.0 + mod[1:2]) + mod[0:1]).astype(BF16) for at in tiles]
    for at, u in zip(tiles, us):
        pa = _dot(u, wa_ref[...])
        qa_ref[at, :] = (pa[:, :aw] * (HEAD_DIM ** -0.5 * LOG2E)).astype(BF16)
        ka_ref[at, :] = pa[:, aw:2 * aw].astype(BF16)
        va_ref[at, :] = pa[:, 2 * aw:].astype(BF16)
        pg = _dot(u, wg_ref[...])
        qg_ref[at, :] = pg[:, :kw] * GLA_DK ** -0.5
        kg_ref[at, :] = pg[:, kw:2 * kw]
        vg_ref[at, :] = pg[:, 2 * kw:2 * kw + vw].astype(BF16)
        gg_ref[at, :] = pg[:, 2 * kw + vw:].astype(BF16)
        z = _dot3(_dot(u, wr_ref[...]), wgate_ref[...]) + bgate_ref[...]
        la_ref[at, :] = (jnp.minimum(z, 0.0) - jnp.log1p(jnp.exp(-jnp.abs(z)))) * (1.0 / GLA_TAU)


def _in_proj(x, mod, w_a, w_g, w_r, w_gate, b_gate):
    bsz, s, d = x.shape
    tm = min(INPROJ_ROWS, s)
    assert s % tm == 0 and tm % min(tm, ROW_TILE) == 0
    row = lambda b, i: (b, i, 0)
    const = lambda b, i: (0, 0)
    out = lambda w, dt: (pl.BlockSpec((None, tm, w), row), jax.ShapeDtypeStruct((bsz, s, w), dt))
    outs = [out(ATTN_WIDTH, BF16)] * 3 + [out(GLA_K_WIDTH, F32)] * 2 + [out(GLA_V_WIDTH, BF16)] * 2 \
        + [out(GLA_K_WIDTH, F32)]
    return pl.pallas_call(
        _inproj_kernel,
        grid=(bsz, s // tm),
        in_specs=[
            pl.BlockSpec((None, tm, d), row),
            pl.BlockSpec((None, 6, d), lambda b, i: (b, 0, 0)),
            pl.BlockSpec(w_a.shape, const),
            pl.BlockSpec(w_g.shape, const),
            pl.BlockSpec(w_r.shape, const),
            pl.BlockSpec(w_gate.shape, const),
            pl.BlockSpec(b_gate.shape, const),
        ],
        out_specs=[o[0] for o in outs],
        out_shape=[o[1] for o in outs],
        compiler_params=_params("parallel", "parallel"),
        name="in_proj",
    )(x, mod, w_a, w_g, w_r, w_gate, b_gate)


def _fold_lanes(x, op):
    parts = [x[:, t * LANES:(t + 1) * LANES] for t in range(x.shape[1] // LANES)]
    while len(parts) > 1:
        parts = [op(parts[t], parts[t + 1]) for t in range(0, len(parts), 2)]
    return parts[0]


def _moba_kernel(q_ref, k_ref, v_ref, g_ref, o_ref, km_ref, kidt_ref, vone_ref, qown_ref, qpast_ref,
                 s_ref, sda_ref, sdb_ref, mrun_ref, acc_ref, *, nb):
    blk = MOBA_BLOCK
    tq = 2 * blk
    half = LANES // 2
    nbp = km_ref.shape[0]
    nt = q_ref.shape[0] // tq
    head0 = lax.broadcasted_iota(jnp.int32, (tq, LANES), 1) < half

    def prepare_queries(t):
        q2 = q_ref[pl.ds(pl.multiple_of(t * tq, tq), tq), :].astype(F32)
        zero = jnp.zeros_like(q2)
        q_own = (jnp.where(head0, q2, zero).astype(BF16), jnp.where(head0, zero, q2).astype(BF16))
        km_hi, km_lo = _split_bf16(km_ref[...])
        blk_id = lax.broadcasted_iota(jnp.int32, (nbp, tq), 0).astype(F32)
        second = jnp.where(lax.broadcasted_iota(jnp.int32, (nbp, tq), 1) >= blk, 1.0, 0.0)
        past = blk_id < (2 * t).astype(F32) + second

        def choose(qm):
            cur = jnp.where(past, _dot_nt(km_hi, qm) + _dot_nt(km_lo, qm), NEG)
            sel = jnp.zeros_like(cur)
            for _ in range(MOBA_TOPK):
                mx = jnp.max(cur, axis=0, keepdims=True)
                first = jnp.min(jnp.where(cur == mx, blk_id, float(nbp)), axis=0, keepdims=True)
                hit = blk_id == first
                sel = jnp.where(hit & past, 1.0, sel)
                cur = jnp.where(hit, -jnp.inf, cur)
            return jnp.where(sel > 0.0, 0.0, NEG)

        fill = jnp.zeros((half - nbp, tq), F32)
        bias = jnp.concatenate([choose(q_own[1]), fill, choose(q_own[0]), fill], axis=0).T
        qown_ref[0] = q_own[0]
        qown_ref[1] = q_own[1]
        qpast_ref[0] = jnp.where(head0, q2, bias).astype(BF16)
        qpast_ref[1] = jnp.where(head0, bias, q2).astype(BF16)

    km_ref[...] = jnp.zeros_like(km_ref)
    lane_k = lax.broadcasted_iota(jnp.int32, (blk, LANES), 1)
    h0 = lane_k < half
    one = jnp.ones((blk, LANES), F32)
    zero_k = jnp.zeros((blk, LANES), F32)
    for j in range(nb):
        at = pl.ds(j * blk, blk)
        kb = k_ref[at, :].astype(F32)
        vb = v_ref[at, :]
        km_ref[pl.ds(j, 1), :] = jnp.sum(kb, axis=0, keepdims=True) * (1.0 / blk)
        cols_j = pl.ds((j % 2) * blk, blk)
        kidt_ref[0, j // 2, :, cols_j] = jnp.where(
            h0, kb, jnp.where(lane_k == half + j, one, zero_k)).T.astype(BF16)
        kidt_ref[1, j // 2, :, cols_j] = jnp.where(
            h0, jnp.where(lane_k == j, one, zero_k), kb).T.astype(BF16)
        vone_ref[0, at, :] = jnp.where(h0, vb, one.astype(BF16))
        vone_ref[1, at, :] = jnp.where(h0, one.astype(BF16), vb)
    prepare_queries(jnp.int32(0))

    rows = lax.broadcasted_iota(jnp.int32, (blk, blk), 0)
    cols = lax.broadcasted_iota(jnp.int32, (blk, blk), 1)
    causal = cols <= rows
    halves = (pl.ds(0, blk), pl.ds(blk, blk))
    same_head = ((lax.broadcasted_iota(jnp.int32, (LANES, LANES), 0) < half)
                 == (lax.broadcasted_iota(jnp.int32, (LANES, LANES), 1) < half))
    avg = jnp.where(same_head, 1.0 / HEAD_DIM, 0.0).astype(BF16)

    def finish_tile(t):
        acc = (acc_ref[0], acc_ref[1])
        row_sum = pltpu.roll(jnp.where(head0, acc[1], acc[0]), half, 1)
        o = jnp.where(head0, acc[0], acc[1]) / row_sum
        sq_hi, sq_lo = _split_bf16(o * o)
        ms = _dot(sq_hi, avg) + _dot(sq_lo, avg)
        o_ref[pl.ds(pl.multiple_of(t * tq, tq), tq), :] = (
            o * lax.rsqrt(ms + EPS) * g_ref[...]).astype(BF16)

    acc_ref[...] = jnp.ones_like(acc_ref)

    def query_tile(t, carry):
        finish_tile(jnp.maximum(t - 1, 0))
        q_own = (qown_ref[0], qown_ref[1])
        q_past = (qpast_ref[0], qpast_ref[1])

        row_a = pl.ds(pl.multiple_of(t * tq, tq), blk)
        own = pl.ds(pl.multiple_of(t * tq, tq), tq)
        for h in range(2):
            kt = kidt_ref[h, t]
            s_a = jnp.where(causal, _dot(q_own[h][:blk], kt[:, :blk]), NEG)
            s_b = jnp.concatenate([_dot(q_past[h][blk:], kt[:, :blk]),
                                   jnp.where(causal, _dot(q_own[h][blk:], kt[:, blk:]), NEG)], axis=1)
            sda_ref[h] = s_a
            sdb_ref[h] = s_b
            mrun_ref[h, halves[0], :] = _fold_lanes(s_a, jnp.maximum)
            mrun_ref[h, halves[1], :] = _fold_lanes(s_b, jnp.maximum)

        def for_each_key_pair(body):
            def four(j4, c):
                for i in range(4):
                    body(4 * j4 + i)
                return c

            lax.fori_loop(0, lax.shift_right_logical(t, 2), four, 0)
            done = t & ~3

            @pl.when((t & 2) == 2)
            def _():
                body(done)
                body(done + 1)

            @pl.when((t & 1) == 1)
            def _():
                body(t - 1)

        def score_pair(jj):
            for h in range(2):
                s = _dot(q_past[h], kidt_ref[h, jj])
                s_ref[h, jj] = s
                mrun_ref[h] = jnp.maximum(mrun_ref[h], _fold_lanes(s, jnp.maximum))

        for_each_key_pair(score_pair)

        prepare_queries(jnp.minimum(t + 1, nt - 1))

        m = [[jnp.max(mrun_ref[h, r, :], axis=-1, keepdims=True) for r in halves] for h in range(2)]
        for h in range(2):
            p_a = jnp.exp2(sda_ref[h] - m[h][0])
            p_b = jnp.exp2(sdb_ref[h] - m[h][1])
            acc_ref[h, halves[0], :] = _dot(p_a.astype(BF16), vone_ref[h, row_a, :])
            acc_ref[h, halves[1], :] = _dot(p_b.astype(BF16), vone_ref[h, own, :])

        def pv_pair(jj):
            at = pl.ds(pl.multiple_of(jj * tq, tq), tq)
            for h in range(2):
                p = jnp.concatenate(
                    [jnp.exp2(s_ref[h, jj, r, :] - m[h][e]) for e, r in enumerate(halves)], axis=0)
                acc_ref[h] += _dot(p.astype(BF16), vone_ref[h, at, :])

        for_each_key_pair(pv_pair)
        return carry

    lax.fori_loop(0, nt, query_tile, 0)
    finish_tile(nt - 1)


def _moba(qa, ka, va, g_pairs):
    bsz, s, w = qa.shape
    blk = MOBA_BLOCK
    tq = 2 * blk
    assert s % tq == 0 and w % LANES == 0
    nb = s // blk
    nbp = -(-nb // 8) * 8
    assert nbp <= LANES // 2
    npairs = w // LANES
    seq = pl.BlockSpec((None, s, LANES), lambda b, p: (b, 0, p))
    return pl.pallas_call(
        functools.partial(_moba_kernel, nb=nb),
        grid=(bsz, npairs),
        in_specs=[seq, seq, seq, pl.BlockSpec((None, 1, LANES), lambda b, p: (p, 0, 0))],
        out_specs=seq,
        out_shape=jax.ShapeDtypeStruct((bsz, s, w), BF16),
        scratch_shapes=[
            pltpu.VMEM((nbp, LANES), F32),
            pltpu.VMEM((2, s // tq, LANES, tq), BF16),
            pltpu.VMEM((2, s, LANES), BF16),
            pltpu.VMEM((2, tq, LANES), BF16),
            pltpu.VMEM((2, tq, LANES), BF16),
            pltpu.VMEM((2, max(s // tq - 1, 1), tq, tq), F32),
            pltpu.VMEM((2, blk, blk), F32),
            pltpu.VMEM((2, blk, tq), F32),
            pltpu.VMEM((2, tq, LANES), F32),
            pltpu.VMEM((2, tq, LANES), F32),
        ],
        compiler_params=_params("parallel", "parallel"),
        name="moba",
    )(qa, ka, va, g_pairs)


def _gla_kernel(q_ref, k_ref, la_ref, v_ref, gg_ref, g_ref, o_ref):
    c = GLA_CHUNK
    dv = GLA_DV
    half = LANES // 2
    tr = min(GLA_ROWS, q_ref.shape[0])
    shift = c.bit_length() - 1
    grp = min(tr, GLA_SUM_ROWS)
    rows = lax.broadcasted_iota(jnp.int32, (grp, grp), 0)
    cols = lax.broadcasted_iota(jnp.int32, (grp, grp), 1)
    same_chunk = lax.shift_right_logical(rows, shift) == lax.shift_right_logical(cols, shift)
    upto = jnp.where(same_chunk & (cols <= rows), 1.0, 0.0).astype(BF16)
    head0 = lax.broadcasted_iota(jnp.int32, (tr, LANES), 1) < half
    in_chunk_row = lax.broadcasted_iota(jnp.int32, (2 * c, c), 0) & (c - 1)
    causal = lax.broadcasted_iota(jnp.int32, (2 * c, c), 1) <= in_chunk_row

    def row_tile(t, st):
        r0 = pl.multiple_of(t * tr, tr)
        at_tile = pl.ds(r0, tr)
        b = []
        for g0 in range(0, tr, grp):
            la_hi, la_lo = _split_bf16(la_ref[pl.ds(r0 + g0, grp), :])
            b.append(_dot(upto, la_hi) + _dot(upto, la_lo))
        b = jnp.concatenate(b, axis=0)
        b_end = jnp.broadcast_to(b.reshape(tr // c, c, LANES)[:, c - 1:c, :],
                                 (tr // c, c, LANES)).reshape(tr, LANES)
        kk = k_ref[at_tile, :]
        q_dec = q_ref[at_tile, :] * jnp.exp(b)
        k_dec = (kk * jnp.exp(-b)).astype(BF16)
        k_end = (kk * jnp.exp(b_end - b)).astype(BF16)
        decay = jnp.exp(b_end)
        zero = jnp.zeros_like(q_dec)
        q_h0 = jnp.where(head0, q_dec, zero).astype(BF16)
        q_h1 = jnp.where(head0, zero, q_dec).astype(BF16)

        outs = []
        for n in range(tr // c):
            at = slice(n * c, (n + 1) * c)
            qs = jnp.concatenate([q_h0[at], q_h1[at]], axis=0)
            attn = jnp.where(causal, _dot_nt(qs, k_dec[at]), 0.0)
            v_c = v_ref[pl.ds(r0 + n * c, c), :]
            res = _dot(attn.astype(BF16), v_c) + _dot_nt(qs, st.astype(BF16))
            outs.append(jnp.concatenate([res[:c, :dv], res[c:, dv:]], axis=1))
            st = st * decay[n * c:n * c + 1, :] + _dot_tn(v_c, k_end[at])
        o = jnp.concatenate(outs, axis=0)

        for h in range(2):
            head = slice(h * dv, (h + 1) * dv)
            o_h = o[:, head]
            ms = jnp.mean(o_h * o_h, axis=-1, keepdims=True)
            gate = gg_ref[at_tile, head].astype(F32)
            y = o_h * lax.rsqrt(ms + EPS) * g_ref[:, head] * jax.nn.silu(gate)
            o_ref[at_tile, head] = y.astype(BF16)
        return st

    lax.fori_loop(0, q_ref.shape[0] // tr, row_tile, jnp.zeros((2 * dv, LANES), F32))


def _gla(qg, kg, la, vg, gg, g_pairs):
    bsz, s, kw = qg.shape
    vw = vg.shape[-1]
    tr = min(GLA_ROWS, s)
    assert s % tr == 0 and tr % min(tr, GLA_SUM_ROWS) == 0 and tr % GLA_CHUNK == 0
    assert GLA_CHUNK & (GLA_CHUNK - 1) == 0
    npairs = kw // LANES
    kspec = pl.BlockSpec((None, s, LANES), lambda b, p: (b, 0, p))
    vspec = pl.BlockSpec((None, s, 2 * GLA_DV), lambda b, p: (b, 0, p))
    return pl.pallas_call(
        _gla_kernel,
        grid=(bsz, npairs),
        in_specs=[kspec, kspec, kspec, vspec, vspec,
                  pl.BlockSpec((None, 1, 2 * GLA_DV), lambda b, p: (p, 0, 0))],
        out_specs=vspec,
        out_shape=jax.ShapeDtypeStruct((bsz, s, vw), BF16),
        compiler_params=_params("parallel", "parallel"),
        name="gla",
    )(qg, kg, la, vg, gg, g_pairs)


def _mix_ffn_kernel(oa_ref, oah_ref, og_ref, ogh_ref, x_ref, xh_ref, mod_ref, woa_ref, wog_ref,
                    g1_ref, b1_ref, wup_ref, bup_ref, cw_ref, cb_ref, wd_ref, g2_ref, b2_ref,
                    o_ref, x1_ref, a_ref):
    tm = x_ref.shape[0]
    halo = xh_ref.shape[0]
    d_ff = wd_ref.shape[0]
    mod = mod_ref[...]
    oa = jnp.concatenate([oah_ref[...], oa_ref[...]], axis=0)
    og = jnp.concatenate([ogh_ref[...], og_ref[...]], axis=0)
    xc = jnp.concatenate([xh_ref[...], x_ref[...]], axis=0)
    y = _dot(oa, woa_ref[...]) + _dot(og, wog_ref[...])
    x1 = _layer_norm(ALPHA * xc + mod[2:3] * y) * g1_ref[...] + b1_ref[...]
    ucat = (_layer_norm(x1) * (1.0 + mod[4:5]) + mod[3:4]).astype(BF16)
    x1_ref[...] = x1[halo:]
    u = ucat[halo:]
    rowid = lax.broadcasted_iota(jnp.int32, (halo + tm, 1), 0)
    keep = (rowid >= halo) | (pl.program_id(1) > 0)
    for c0 in range(0, d_ff, FF_CHUNK):
        val = slice(c0, c0 + FF_CHUNK)
        gate = slice(d_ff + c0, d_ff + c0 + FF_CHUNK)
        hv = _dot(u, wup_ref[:, val]) + bup_ref[:, val]
        hg = jnp.where(keep, _dot(ucat, wup_ref[:, gate]) + bup_ref[:, gate], 0.0)
        conv = cb_ref[:, val]
        for tap in range(CONV_WIDTH):
            back = CONV_WIDTH - 1 - tap
            conv = conv + cw_ref[tap:tap + 1, val] * hg[halo - back:halo - back + tm]
        a_ref[:, val] = (0.5 * conv * (1.0 + lax.erf(conv * (0.5 ** 0.5))) * hv).astype(BF16)
    h2 = ALPHA * x1_ref[...] + mod[5:6] * _dot(a_ref[...], wd_ref[...])
    o_ref[...] = _layer_norm(h2) * g2_ref[...] + b2_ref[...]


def _mix_ffn(oa, og, x, mod, w_oa, w_og, ln1_g, ln1_b, w_up, b_up, conv_w, conv_b, w_d, ln2_g, ln2_b):
    bsz, s, d = x.shape
    d_ff = w_d.shape[0]
    tm = min(FFN_ROWS, s)
    halo = CONV_HALO
    assert s % tm == 0 and tm % halo == 0 and halo >= CONV_WIDTH - 1 and d_ff % FF_CHUNK == 0
    row = lambda b, i: (b, i, 0)
    prev = lambda b, i: (b, jnp.maximum(i * (tm // halo) - 1, 0), 0)
    const = lambda b, i: (0, 0)
    tile = lambda a: [pl.BlockSpec((None, tm, a.shape[-1]), row),
                      pl.BlockSpec((None, halo, a.shape[-1]), prev)]
    consts = (w_oa, w_og, ln1_g, ln1_b, w_up, b_up, conv_w, conv_b, w_d, ln2_g, ln2_b)
    return pl.pallas_call(
        _mix_ffn_kernel,
        grid=(bsz, s // tm),
        in_specs=tile(oa) + tile(og) + tile(x) + [pl.BlockSpec((None, 6, d), lambda b, i: (b, 0, 0))]
        + [pl.BlockSpec(a.shape, const) for a in consts],
        out_specs=pl.BlockSpec((None, tm, d), row),
        out_shape=jax.ShapeDtypeStruct((bsz, s, d), F32),
        scratch_shapes=[pltpu.VMEM((tm, d), F32), pltpu.VMEM((tm, d_ff), BF16)],
        compiler_params=_params("parallel", "parallel"),
        name="mix_ffn",
    )(oa, oa, og, og, x, x, mod, *consts)


def _layer(x, mod, w_in, w_gla_gate, b_gla_gate, attn_norm_g, gla_norm_g, w_o, ln1_g, ln1_b,
           w_up, b_up, conv_w, conv_b, w_down, ln2_g, ln2_b):
    d = x.shape[-1]
    d_ff = w_down.shape[0]
    a3 = 3 * ATTN_WIDTH
    g_end = a3 + 2 * GLA_K_WIDTH + 2 * GLA_V_WIDTH
    w_a = w_in[:, :a3].astype(BF16)
    w_g = w_in[:, a3:g_end].astype(BF16)
    w_r = jnp.pad(w_in[:, g_end:], ((0, 0), (0, LANES - GLA_GATE_RANK))).astype(BF16)
    w_gate = jnp.pad(w_gla_gate, ((0, LANES - GLA_GATE_RANK), (0, 0)))
    qa, ka, va, qg, kg, vg, gg, la = _in_proj(x, mod, w_a, w_g, w_r, w_gate,
                                              b_gla_gate.reshape(1, -1))
    oa = _moba(qa, ka, va, attn_norm_g.reshape(-1, 1, LANES))
    og = _gla(qg, kg, la, vg, gg, gla_norm_g.reshape(-1, 1, 2 * GLA_DV))
    w_ob = w_o.astype(BF16)
    return _mix_ffn(oa, og, x, mod, w_ob[:ATTN_WIDTH], w_ob[ATTN_WIDTH:],
                    ln1_g.reshape(1, d), ln1_b.reshape(1, d), w_up.astype(BF16), b_up.reshape(1, -1),
                    conv_w, conv_b.reshape(1, -1), w_down.astype(BF16),
                    ln2_g.reshape(1, d), ln2_b.reshape(1, d))


def kernel(x, c, w_ada, b_ada, w_in, w_gla_gate, b_gla_gate, attn_norm_g, gla_norm_g, w_o,
           ln1_g, ln1_b, w_up, b_up, conv_w, conv_b, w_down, ln2_g, ln2_b):
    bsz, _, d = x.shape
    for l in range(w_in.shape[0]):
        mod = _adaln_mod(c, w_ada[l], b_ada[l]).reshape(bsz, 6, d)
        x = _layer(x, mod, w_in[l], w_gla_gate[l], b_gla_gate[l], attn_norm_g[l], gla_norm_g[l],
                   w_o[l], ln1_g[l], ln1_b[l], w_up[l], b_up[l], conv_w[l], conv_b[l], w_down[l],
                   ln2_g[l], ln2_b[l])
    return x
```

```python
import functools

import jax
import jax.numpy as jnp
from jax import lax
from jax.experimental import pallas as pl
from jax.experimental.pallas import tpu as pltpu

F32 = jnp.float32
BF16 = jnp.bfloat16

HEAD_DIM = 64
N_ATTN_HEADS = 8
ATTN_WIDTH = N_ATTN_HEADS * HEAD_DIM
N_GLA_HEADS = 4
GLA_DK = 64
GLA_DV = 128
GLA_K_WIDTH = N_GLA_HEADS * GLA_DK
GLA_V_WIDTH = N_GLA_HEADS * GLA_DV
GLA_GATE_RANK = 16
GLA_TAU = 16.0
GLA_CHUNK = 64
MOBA_BLOCK = 256
MOBA_TOPK = 3
CONV_WIDTH = 3
DEPTH = 1
ALPHA = (2.0 * DEPTH) ** 0.25
EPS = 1e-5
NEG = -1e30
LOG2E = 1.4426950408889634

LANES = 128
BF16_ROWS = 16
VMEM_LIMIT = 56 * 1024 * 1024

ROW_TILE = 512
FFN_ROWS = 1024
INPROJ_ROWS = 1024
GLA_ROWS = 2048
GLA_SUM_ROWS = 256
FF_CHUNK = 256
CONV_HALO = BF16_ROWS


def _dot(a, b):
    return jnp.dot(a, b, preferred_element_type=F32)


def _dot_nt(a, b):
    return lax.dot_general(a, b, (((1,), (1,)), ((), ())), preferred_element_type=F32)


def _dot_tn(a, b):
    return lax.dot_general(a, b, (((0,), (0,)), ((), ())), preferred_element_type=F32)


def _split_bf16(a):
    hi = a.astype(BF16)
    lo = (a - hi.astype(F32)).astype(BF16)
    return hi, lo


def _dot3(a, b):
    ah, al = _split_bf16(a)
    bh, bl = _split_bf16(b)
    return _dot(ah, bh) + (_dot(ah, bl) + _dot(al, bh))


def _layer_norm(x):
    mu = jnp.mean(x, -1, keepdims=True)
    xc = x - mu
    var = jnp.mean(xc * xc, -1, keepdims=True)
    return xc * lax.rsqrt(var + EPS)


def _params(*sem):
    return pltpu.CompilerParams(dimension_semantics=sem, vmem_limit_bytes=VMEM_LIMIT)


def _mod_kernel(c_ref, w_ref, b_ref, o_ref):
    o_ref[...] = _dot3(jax.nn.silu(c_ref[...]), w_ref[...]) + b_ref[...]


def _adaln_mod(c, w, b):
    bsz, d = c.shape
    n = w.shape[1]
    return pl.pallas_call(
        _mod_kernel,
        grid=(n // d,),
        in_specs=[
            pl.BlockSpec((bsz, d), lambda j: (0, 0)),
            pl.BlockSpec((d, d), lambda j: (0, j)),
            pl.BlockSpec((1, d), lambda j: (0, j)),
        ],
        out_specs=pl.BlockSpec((bsz, d), lambda j: (0, j)),
        out_shape=jax.ShapeDtypeStruct((bsz, n), F32),
        compiler_params=_params("parallel"),
        name="adaln_mod",
    )(c, w, b.reshape(1, n))


def _inproj_kernel(x_ref, mod_ref, wa_ref, wg_ref, wr_ref, wgate_ref, bgate_ref,
                   qa_ref, ka_ref, va_ref, qg_ref, kg_ref, vg_ref, gg_ref, la_ref):
    mod = mod_ref[...]
    aw, kw, vw = ATTN_WIDTH, GLA_K_WIDTH, GLA_V_WIDTH
    sub = min(x_ref.shape[0], ROW_TILE)
    tiles = [pl.ds(r0, sub) for r0 in range(0, x_ref.shape[0], sub)]
    us = [(_layer_norm(x_ref[at, :]) * (1.0 + mod[1:2]) + mod[0:1]).astype(BF16) for at in tiles]
    for at, u in zip(tiles, us):
        pa = _dot(u, wa_ref[...])
        qa_ref[at, :] = (pa[:, :aw] * (HEAD_DIM ** -0.5 * LOG2E)).astype(BF16)
        ka_ref[at, :] = pa[:, aw:2 * aw].astype(BF16)
        va_ref[at, :] = pa[:, 2 * aw:].astype(BF16)
        pg = _dot(u, wg_ref[...])
        qg_ref[at, :] = pg[:, :kw] * GLA_DK ** -0.5
        kg_ref[at, :] = pg[:, kw:2 * kw]
        vg_ref[at, :] = pg[:, 2 * kw:2 * kw + vw].astype(BF16)
        gg_ref[at, :] = pg[:, 2 * kw + vw:].astype(BF16)
        z = _dot3(_dot(u, wr_ref[...]), wgate_ref[...]) + bgate_ref[...]
        la_ref[at, :] = (jnp.minimum(z, 0.0) - jnp.log1p(jnp.exp(-jnp.abs(z)))) * (1.0 / GLA_TAU)


def _in_proj(x, mod, w_a, w_g, w_r, w_gate, b_gate):
    bsz, s, d = x.shape
    tm = min(INPROJ_ROWS, s)
    assert s % tm == 0 and tm % min(tm, ROW_TILE) == 0
    row = lambda b, i: (b, i, 0)
    const = lambda b, i: (0, 0)
    out = lambda w, dt: (pl.BlockSpec((None, tm, w), row), jax.ShapeDtypeStruct((bsz, s, w), dt))
    outs = [out(ATTN_WIDTH, BF16)] * 3 + [out(GLA_K_WIDTH, F32)] * 2 + [out(GLA_V_WIDTH, BF16)] * 2 \
        + [out(GLA_K_WIDTH, F32)]
    return pl.pallas_call(
        _inproj_kernel,
        grid=(bsz, s // tm),
        in_specs=[
            pl.BlockSpec((None, tm, d), row),
            pl.BlockSpec((None, 6, d), lambda b, i: (b, 0, 0)),
            pl.BlockSpec(w_a.shape, const),
            pl.BlockSpec(w_g.shape, const),
            pl.BlockSpec(w_r.shape, const),
            pl.BlockSpec(w_gate.shape, const),
            pl.BlockSpec(b_gate.shape, const),
        ],
        out_specs=[o[0] for o in outs],
        out_shape=[o[1] for o in outs],
        compiler_params=_params("parallel", "parallel"),
        name="in_proj",
    )(x, mod, w_a, w_g, w_r, w_gate, b_gate)


def _fold_lanes(x, op):
    parts = [x[:, t * LANES:(t + 1) * LANES] for t in range(x.shape[1] // LANES)]
    while len(parts) > 1:
        parts = [op(parts[t], parts[t + 1]) for t in range(0, len(parts), 2)]
    return parts[0]


def _moba_kernel(q_ref, k_ref, v_ref, g_ref, o_ref, km_ref, kidt_ref, vone_ref, qown_ref, qpast_ref,
                 s_ref, sda_ref, sdb_ref, mrun_ref, acc_ref, *, nb):
    blk = MOBA_BLOCK
    tq = 2 * blk
    half = LANES // 2
    nbp = km_ref.shape[0]
    nt = q_ref.shape[0] // tq
    head0 = lax.broadcasted_iota(jnp.int32, (tq, LANES), 1) < half

    def prepare_queries(t):
        q2 = q_ref[pl.ds(pl.multiple_of(t * tq, tq), tq), :].astype(F32)
        zero = jnp.zeros_like(q2)
        q_own = (jnp.where(head0, q2, zero).astype(BF16), jnp.where(head0, zero, q2).astype(BF16))
        km_hi, km_lo = _split_bf16(km_ref[...])
        blk_id = lax.broadcasted_iota(jnp.int32, (nbp, tq), 0).astype(F32)
        second = jnp.where(lax.broadcasted_iota(jnp.int32, (nbp, tq), 1) >= blk, 1.0, 0.0)
        past = blk_id < (2 * t).astype(F32) + second

        def choose(qm):
            cur = jnp.where(past, _dot_nt(km_hi, qm) + _dot_nt(km_lo, qm), NEG)
            sel = jnp.zeros_like(cur)
            for _ in range(MOBA_TOPK):
                mx = jnp.max(cur, axis=0, keepdims=True)
                first = jnp.min(jnp.where(cur == mx, blk_id, float(nbp)), axis=0, keepdims=True)
                hit = blk_id == first
                sel = jnp.where(hit & past, 1.0, sel)
                cur = jnp.where(hit, -jnp.inf, cur)
            return jnp.where(sel > 0.0, 0.0, NEG)

        fill = jnp.zeros((half - nbp, tq), F32)
        bias = jnp.concatenate([choose(q_own[1]), fill, choose(q_own[0]), fill], axis=0).T
        qown_ref[0] = q_own[0]
        qown_ref[1] = q_own[1]
        qpast_ref[0] = jnp.where(head0, q2, bias).astype(BF16)
        qpast_ref[1] = jnp.where(head0, bias, q2).astype(BF16)

    km_ref[...] = jnp.zeros_like(km_ref)
    lane_k = lax.broadcasted_iota(jnp.int32, (blk, LANES), 1)
    h0 = lane_k < half
    one = jnp.ones((blk, LANES), F32)
    zero_k = jnp.zeros((blk, LANES), F32)
    for j in range(nb):
        at = pl.ds(j * blk, blk)
        kb = k_ref[at, :].astype(F32)
        vb = v_ref[at, :]
        km_ref[pl.ds(j, 1), :] = jnp.sum(kb, axis=0, keepdims=True) * (1.0 / blk)
        cols_j = pl.ds((j % 2) * blk, blk)
        kidt_ref[0, j // 2, :, cols_j] = jnp.where(
            h0, kb, jnp.where(lane_k == half + j, one, zero_k)).T.astype(BF16)
        kidt_ref[1, j // 2, :, cols_j] = jnp.where(
            h0, jnp.where(lane_k == j, one, zero_k), kb).T.astype(BF16)
        vone_ref[0, at, :] = jnp.where(h0, vb, one.astype(BF16))
        vone_ref[1, at, :] = jnp.where(h0, one.astype(BF16), vb)
    prepare_queries(jnp.int32(0))

    rows = lax.broadcasted_iota(jnp.int32, (blk, blk), 0)
    cols = lax.broadcasted_iota(jnp.int32, (blk, blk), 1)
    causal = cols <= rows
    halves = (pl.ds(0, blk), pl.ds(blk, blk))
    same_head = ((lax.broadcasted_iota(jnp.int32, (LANES, LANES), 0) < half)
                 == (lax.broadcasted_iota(jnp.int32, (LANES, LANES), 1) < half))
    avg = jnp.where(same_head, 1.0 / HEAD_DIM, 0.0).astype(BF16)

    def finish_tile(t):
        acc = (acc_ref[0], acc_ref[1])
        row_sum = pltpu.roll(jnp.where(head0, acc[1], acc[0]), half, 1)
        o = jnp.where(head0, acc[0], acc[1]) / row_sum
        sq_hi, sq_lo = _split_bf16(o * o)
        ms = _dot(sq_hi, avg) + _dot(sq_lo, avg)
        o_ref[pl.ds(pl.multiple_of(t * tq, tq), tq), :] = (
            o * lax.rsqrt(ms + EPS) * g_ref[...]).astype(BF16)

    acc_ref[...] = jnp.ones_like(acc_ref)

    def query_tile(t, carry):
        finish_tile(jnp.maximum(t - 1, 0))
        q_own = (qown_ref[0], qown_ref[1])
        q_past = (qpast_ref[0], qpast_ref[1])

        row_a = pl.ds(pl.multiple_of(t * tq, tq), blk)
        own = pl.ds(pl.multiple_of(t * tq, tq), tq)
        for h in range(2):
            kt = kidt_ref[h, t]
            s_a = jnp.where(causal, _dot(q_own[h][:blk], kt[:, :blk]), NEG)
            s_b = jnp.concatenate([_dot(q_past[h][blk:], kt[:, :blk]),
                                   jnp.where(causal, _dot(q_own[h][blk:], kt[:, blk:]), NEG)], axis=1)
            sda_ref[h] = s_a
            sdb_ref[h] = s_b
            mrun_ref[h, halves[0], :] = _fold_lanes(s_a, jnp.maximum)
            mrun_ref[h, halves[1], :] = _fold_lanes(s_b, jnp.maximum)

        def for_each_key_group(body):
            def four(j4, c):
                body(4 * j4, 4)
                return c

            lax.fori_loop(0, lax.shift_right_logical(t, 2), four, 0)
            pl.when((t & 2) == 2)(lambda: body(t & ~3, 2))
            pl.when((t & 1) == 1)(lambda: body(t - 1, 1))

        def score_group(jj0, n):
            for h in range(2):
                best = None
                for i in range(n):
                    s = _dot(q_past[h], kidt_ref[h, jj0 + i])
                    s_ref[h, jj0 + i] = s
                    fold = _fold_lanes(s, jnp.maximum)
                    best = fold if best is None else jnp.maximum(best, fold)
                mrun_ref[h] = jnp.maximum(mrun_ref[h], best)

        for_each_key_group(score_group)

        prepare_queries(jnp.minimum(t + 1, nt - 1))

        m = [[jnp.max(mrun_ref[h, r, :], axis=-1, keepdims=True) for r in halves] for h in range(2)]
        for h in range(2):
            p_a = jnp.exp2(sda_ref[h] - m[h][0])
            p_b = jnp.exp2(sdb_ref[h] - m[h][1])
            acc_ref[h, halves[0], :] = _dot(p_a.astype(BF16), vone_ref[h, row_a, :])
            acc_ref[h, halves[1], :] = _dot(p_b.astype(BF16), vone_ref[h, own, :])

        def pv_group(jj0, n):
            at = pl.ds(pl.multiple_of(jj0 * tq, tq), n * tq)
            for h in range(2):
                p = jnp.concatenate([
                    jnp.concatenate([jnp.exp2(s_ref[h, jj0 + i, r, :] - m[h][e])
                                     for e, r in enumerate(halves)], axis=0).astype(BF16)
                    for i in range(n)], axis=1)
                acc_ref[h] += _dot(p, vone_ref[h, at, :])

        for_each_key_group(pv_group)
        return carry

    lax.fori_loop(0, nt, query_tile, 0)
    finish_tile(nt - 1)


def _moba(qa, ka, va, g_pairs):
    bsz, s, w = qa.shape
    blk = MOBA_BLOCK
    tq = 2 * blk
    assert s % tq == 0 and w % LANES == 0
    nb = s // blk
    nbp = -(-nb // 8) * 8
    assert nbp <= LANES // 2
    npairs = w // LANES
    seq = pl.BlockSpec((None, s, LANES), lambda b, p: (b, 0, p))
    return pl.pallas_call(
        functools.partial(_moba_kernel, nb=nb),
        grid=(bsz, npairs),
        in_specs=[seq, seq, seq, pl.BlockSpec((None, 1, LANES), lambda b, p: (p, 0, 0))],
        out_specs=seq,
        out_shape=jax.ShapeDtypeStruct((bsz, s, w), BF16),
        scratch_shapes=[
            pltpu.VMEM((nbp, LANES), F32),
            pltpu.VMEM((2, s // tq, LANES, tq), BF16),
            pltpu.VMEM((2, s, LANES), BF16),
            pltpu.VMEM((2, tq, LANES), BF16),
            pltpu.VMEM((2, tq, LANES), BF16),
            pltpu.VMEM((2, max(s // tq - 1, 1), tq, tq), F32),
            pltpu.VMEM((2, blk, blk), F32),
            pltpu.VMEM((2, blk, tq), F32),
            pltpu.VMEM((2, tq, LANES), F32),
            pltpu.VMEM((2, tq, LANES), F32),
        ],
        compiler_params=_params("parallel", "parallel"),
        name="moba",
    )(qa, ka, va, g_pairs)


def _gla_kernel(q_ref, k_ref, la_ref, v_ref, gg_ref, g_ref, o_ref):
    c = GLA_CHUNK
    dv = GLA_DV
    half = LANES // 2
    tr = min(GLA_ROWS, q_ref.shape[0])
    shift = c.bit_length() - 1
    grp = min(tr, GLA_SUM_ROWS)
    rows = lax.broadcasted_iota(jnp.int32, (grp, grp), 0)
    cols = lax.broadcasted_iota(jnp.int32, (grp, grp), 1)
    same_chunk = lax.shift_right_logical(rows, shift) == lax.shift_right_logical(cols, shift)
    upto = jnp.where(same_chunk & (cols <= rows), 1.0, 0.0).astype(BF16)
    head0 = lax.broadcasted_iota(jnp.int32, (tr, LANES), 1) < half
    in_chunk_row = lax.broadcasted_iota(jnp.int32, (2 * c, c), 0) & (c - 1)
    causal = lax.broadcasted_iota(jnp.int32, (2 * c, c), 1) <= in_chunk_row

    def row_tile(t, st):
        r0 = pl.multiple_of(t * tr, tr)
        at_tile = pl.ds(r0, tr)
        b = []
        for g0 in range(0, tr, grp):
            la_hi, la_lo = _split_bf16(la_ref[pl.ds(r0 + g0, grp), :])
            b.append(_dot(upto, la_hi) + _dot(upto, la_lo))
        b = jnp.concatenate(b, axis=0)
        b_end = jnp.broadcast_to(b.reshape(tr // c, c, LANES)[:, c - 1:c, :],
                                 (tr // c, c, LANES)).reshape(tr, LANES)
        kk = k_ref[at_tile, :]
        q_dec = q_ref[at_tile, :] * jnp.exp(b)
        k_dec = (kk * jnp.exp(-b)).astype(BF16)
        k_end = (kk * jnp.exp(b_end - b)).astype(BF16)
        decay = jnp.exp(b_end)
        zero = jnp.zeros_like(q_dec)
        q_h0 = jnp.where(head0, q_dec, zero).astype(BF16)
        q_h1 = jnp.where(head0, zero, q_dec).astype(BF16)

        outs = []
        for n in range(tr // c):
            at = slice(n * c, (n + 1) * c)
            qs = jnp.concatenate([q_h0[at], q_h1[at]], axis=0)
            attn = jnp.where(causal, _dot_nt(qs, k_dec[at]), 0.0)
            v_c = v_ref[pl.ds(r0 + n * c, c), :]
            res = _dot(attn.astype(BF16), v_c) + _dot_nt(qs, st.astype(BF16))
            outs.append(jnp.concatenate([res[:c, :dv], res[c:, dv:]], axis=1))
            st = st * decay[n * c:n * c + 1, :] + _dot_tn(v_c, k_end[at])
        o = jnp.concatenate(outs, axis=0)

        for h in range(2):
            head = slice(h * dv, (h + 1) * dv)
            o_h = o[:, head]
            ms = jnp.mean(o_h * o_h, axis=-1, keepdims=True)
            gate = gg_ref[at_tile, head].astype(F32)
            y = o_h * lax.rsqrt(ms + EPS) * g_ref[:, head] * jax.nn.silu(gate)
            o_ref[at_tile, head] = y.astype(BF16)
        return st

    lax.fori_loop(0, q_ref.shape[0] // tr, row_tile, jnp.zeros((2 * dv, LANES), F32))


def _gla(qg, kg, la, vg, gg, g_pairs):
    bsz, s, kw = qg.shape
    vw = vg.shape[-1]
    tr = min(GLA_ROWS, s)
    assert s % tr == 0 and tr % min(tr, GLA_SUM_ROWS) == 0 and tr % GLA_CHUNK == 0
    assert GLA_CHUNK & (GLA_CHUNK - 1) == 0
    npairs = kw // LANES
    kspec = pl.BlockSpec((None, s, LANES), lambda b, p: (b, 0, p))
    vspec = pl.BlockSpec((None, s, 2 * GLA_DV), lambda b, p: (b, 0, p))
    return pl.pallas_call(
        _gla_kernel,
        grid=(bsz, npairs),
        in_specs=[kspec, kspec, kspec, vspec, vspec,
                  pl.BlockSpec((None, 1, 2 * GLA_DV), lambda b, p: (p, 0, 0))],
        out_specs=vspec,
        out_shape=jax.ShapeDtypeStruct((bsz, s, vw), BF16),
        compiler_params=_params("parallel", "parallel"),
        name="gla",
    )(qg, kg, la, vg, gg, g_pairs)


def _mix_ffn_kernel(oa_ref, oah_ref, og_ref, ogh_ref, x_ref, xh_ref, mod_ref, woa_ref, wog_ref,
                    g1_ref, b1_ref, wup_ref, bup_ref, cw_ref, cb_ref, wd_ref, g2_ref, b2_ref,
                    o_ref, x1_ref, a_ref):
    tm = x_ref.shape[0]
    halo = xh_ref.shape[0]
    d_ff = wd_ref.shape[0]
    mod = mod_ref[...]
    oa = jnp.concatenate([oah_ref[...], oa_ref[...]], axis=0)
    og = jnp.concatenate([ogh_ref[...], og_ref[...]], axis=0)
    xc = jnp.concatenate([xh_ref[...], x_ref[...]], axis=0)
    y = _dot(oa, woa_ref[...]) + _dot(og, wog_ref[...])
    x1 = _layer_norm(ALPHA * xc + mod[2:3] * y) * g1_ref[...] + b1_ref[...]
    ucat = (_layer_norm(x1) * (1.0 + mod[4:5]) + mod[3:4]).astype(BF16)
    x1_ref[...] = x1[halo:]
    u = ucat[halo:]
    rowid = lax.broadcasted_iota(jnp.int32, (halo + tm, 1), 0)
    keep = (rowid >= halo) | (pl.program_id(1) > 0)
    for c0 in range(0, d_ff, FF_CHUNK):
        val = slice(c0, c0 + FF_CHUNK)
        gate = slice(d_ff + c0, d_ff + c0 + FF_CHUNK)
        hv = _dot(u, wup_ref[:, val]) + bup_ref[:, val]
        hg = jnp.where(keep, _dot(ucat, wup_ref[:, gate]) + bup_ref[:, gate], 0.0)
        conv = cb_ref[:, val]
        for tap in range(CONV_WIDTH):
            back = CONV_WIDTH - 1 - tap
            conv = conv + cw_ref[tap:tap + 1, val] * hg[halo - back:halo - back + tm]
        a_ref[:, val] = (0.5 * conv * (1.0 + lax.erf(conv * (0.5 ** 0.5))) * hv).astype(BF16)
    h2 = ALPHA * x1_ref[...] + mod[5:6] * _dot(a_ref[...], wd_ref[...])
    o_ref[...] = _layer_norm(h2) * g2_ref[...] + b2_ref[...]


def _mix_ffn(oa, og, x, mod, w_oa, w_og, ln1_g, ln1_b, w_up, b_up, conv_w, conv_b, w_d, ln2_g, ln2_b):
    bsz, s, d = x.shape
    d_ff = w_d.shape[0]
    tm = min(FFN_ROWS, s)
    halo = CONV_HALO
    assert s % tm == 0 and tm % halo == 0 and halo >= CONV_WIDTH - 1 and d_ff % FF_CHUNK == 0
    row = lambda b, i: (b, i, 0)
    prev = lambda b, i: (b, jnp.maximum(i * (tm // halo) - 1, 0), 0)
    const = lambda b, i: (0, 0)
    tile = lambda a: [pl.BlockSpec((None, tm, a.shape[-1]), row),
                      pl.BlockSpec((None, halo, a.shape[-1]), prev)]
    consts = (w_oa, w_og, ln1_g, ln1_b, w_up, b_up, conv_w, conv_b, w_d, ln2_g, ln2_b)
    return pl.pallas_call(
        _mix_ffn_kernel,
        grid=(bsz, s // tm),
        in_specs=tile(oa) + tile(og) + tile(x) + [pl.BlockSpec((None, 6, d), lambda b, i: (b, 0, 0))]
        + [pl.BlockSpec(a.shape, const) for a in consts],
        out_specs=pl.BlockSpec((None, tm, d), row),
        out_shape=jax.ShapeDtypeStruct((bsz, s, d), F32),
        scratch_shapes=[pltpu.VMEM((tm, d), F32), pltpu.VMEM((tm, d_ff), BF16)],
        compiler_params=_params("parallel", "parallel"),
        name="mix_ffn",
    )(oa, oa, og, og, x, x, mod, *consts)


def _layer(x, mod, w_in, w_gla_gate, b_gla_gate, attn_norm_g, gla_norm_g, w_o, ln1_g, ln1_b,
           w_up, b_up, conv_w, conv_b, w_down, ln2_g, ln2_b):
    d = x.shape[-1]
    d_ff = w_down.shape[0]
    a3 = 3 * ATTN_WIDTH
    g_end = a3 + 2 * GLA_K_WIDTH + 2 * GLA_V_WIDTH
    w_a = w_in[:, :a3].astype(BF16)
    w_g = w_in[:, a3:g_end].astype(BF16)
    w_r = jnp.pad(w_in[:, g_end:], ((0, 0), (0, LANES - GLA_GATE_RANK))).astype(BF16)
    w_gate = jnp.pad(w_gla_gate, ((0, LANES - GLA_GATE_RANK), (0, 0)))
    qa, ka, va, qg, kg, vg, gg, la = _in_proj(x, mod, w_a, w_g, w_r, w_gate,
                                              b_gla_gate.reshape(1, -1))
    oa = _moba(qa, ka, va, attn_norm_g.reshape(-1, 1, LANES))
    og = _gla(qg, kg, la, vg, gg, gla_norm_g.reshape(-1, 1, 2 * GLA_DV))
    w_ob = w_o.astype(BF16)
    return _mix_ffn(oa, og, x, mod, w_ob[:ATTN_WIDTH], w_ob[ATTN_WIDTH:],
                    ln1_g.reshape(1, d), ln1_b.reshape(1, d), w_up.astype(BF16), b_up.reshape(1, -1),
                    conv_w, conv_b.reshape(1, -1), w_down.astype(BF16),
                    ln2_g.reshape(1, d), ln2_b.reshape(1, d))


def kernel(x, c, w_ada, b_ada, w_in, w_gla_gate, b_gla_gate, attn_norm_g, gla_norm_g, w_o,
           ln1_g, ln1_b, w_up, b_up, conv_w, conv_b, w_down, ln2_g, ln2_b):
    bsz, _, d = x.shape
    for l in range(w_in.shape[0]):
        mod = _adaln_mod(c, w_ada[l], b_ada[l]).reshape(bsz, 6, d)
        x = _layer(x, mod, w_in[l], w_gla_gate[l], b_gla_gate[l], attn_norm_g[l], gla_norm_g[l],
                   w_o[l], ln1_g[l], ln1_b[l], w_up[l], b_up[l], conv_w[l], conv_b[l], w_down[l],
                   ln2_g[l], ln2_b[l])
    return x
```

```python
import functools

import jax
import jax.numpy as jnp
from jax import lax
from jax.experimental import pallas as pl
from jax.experimental.pallas import tpu as pltpu

F32 = jnp.float32
BF16 = jnp.bfloat16

HEAD_DIM = 64
N_ATTN_HEADS = 8
ATTN_WIDTH = N_ATTN_HEADS * HEAD_DIM
N_GLA_HEADS = 4
GLA_DK = 64
GLA_DV = 128
GLA_K_WIDTH = N_GLA_HEADS * GLA_DK
GLA_V_WIDTH = N_GLA_HEADS * GLA_DV
GLA_GATE_RANK = 16
GLA_TAU = 16.0
GLA_CHUNK = 64
MOBA_BLOCK = 256
MOBA_TOPK = 3
CONV_WIDTH = 3
DEPTH = 1
ALPHA = (2.0 * DEPTH) ** 0.25
EPS = 1e-5
NEG = -1e30
LOG2E = 1.4426950408889634

LANES = 128
BF16_ROWS = 16
VMEM_LIMIT = 56 * 1024 * 1024

ROW_TILE = 512
FFN_ROWS = 1024
INPROJ_ROWS = 1024
GLA_ROWS = 2048
GLA_SUM_ROWS = 256
FF_CHUNK = 256
CONV_HALO = BF16_ROWS


def _dot(a, b):
    return jnp.dot(a, b, preferred_element_type=F32)


def _dot_nt(a, b):
    return lax.dot_general(a, b, (((1,), (1,)), ((), ())), preferred_element_type=F32)


def _dot_tn(a, b):
    return lax.dot_general(a, b, (((0,), (0,)), ((), ())), preferred_element_type=F32)


def _split_bf16(a):
    hi = a.astype(BF16)
    lo = (a - hi.astype(F32)).astype(BF16)
    return hi, lo


def _dot3(a, b):
    ah, al = _split_bf16(a)
    bh, bl = _split_bf16(b)
    return _dot(ah, bh) + (_dot(ah, bl) + _dot(al, bh))


def _layer_norm(x):
    mu = jnp.mean(x, -1, keepdims=True)
    xc = x - mu
    var = jnp.mean(xc * xc, -1, keepdims=True)
    return xc * lax.rsqrt(var + EPS)


def _params(*sem):
    return pltpu.CompilerParams(dimension_semantics=sem, vmem_limit_bytes=VMEM_LIMIT)


def _mod_kernel(c_ref, w_ref, b_ref, o_ref):
    o_ref[...] = _dot3(jax.nn.silu(c_ref[...]), w_ref[...]) + b_ref[...]


def _adaln_mod(c, w, b):
    bsz, d = c.shape
    n = w.shape[1]
    return pl.pallas_call(
        _mod_kernel,
        grid=(n // d,),
        in_specs=[
            pl.BlockSpec((bsz, d), lambda j: (0, 0)),
            pl.BlockSpec((d, d), lambda j: (0, j)),
            pl.BlockSpec((1, d), lambda j: (0, j)),
        ],
        out_specs=pl.BlockSpec((bsz, d), lambda j: (0, j)),
        out_shape=jax.ShapeDtypeStruct((bsz, n), F32),
        compiler_params=_params("parallel"),
        name="adaln_mod",
    )(c, w, b.reshape(1, n))


def _inproj_kernel(x_ref, mod_ref, wa_ref, wg_ref, wr_ref, wgate_ref, bgate_ref,
                   qa_ref, ka_ref, va_ref, qg_ref, kg_ref, vg_ref, gg_ref, la_ref):
    mod = mod_ref[...]
    aw, kw, vw = ATTN_WIDTH, GLA_K_WIDTH, GLA_V_WIDTH
    sub = min(x_ref.shape[0], ROW_TILE)
    tiles = [pl.ds(r0, sub) for r0 in range(0, x_ref.shape[0], sub)]
    us = [(_layer_norm(x_ref[at, :]) * (1.0 + mod[1:2]) + mod[0:1]).astype(BF16) for at in tiles]
    for at, u in zip(tiles, us):
        pa = _dot(u, wa_ref[...])
        qa_ref[at, :] = (pa[:, :aw] * (HEAD_DIM ** -0.5 * LOG2E)).astype(BF16)
        ka_ref[at, :] = pa[:, aw:2 * aw].astype(BF16)
        va_ref[at, :] = pa[:, 2 * aw:].astype(BF16)
        pg = _dot(u, wg_ref[...])
        qg_ref[at, :] = pg[:, :kw] * GLA_DK ** -0.5
        kg_ref[at, :] = pg[:, kw:2 * kw]
        vg_ref[at, :] = pg[:, 2 * kw:2 * kw + vw].astype(BF16)
        gg_ref[at, :] = pg[:, 2 * kw + vw:].astype(BF16)
        z = _dot3(_dot(u, wr_ref[...]), wgate_ref[...]) + bgate_ref[...]
        la_ref[at, :] = (jnp.minimum(z, 0.0) - jnp.log1p(jnp.exp(-jnp.abs(z)))) * (1.0 / GLA_TAU)


def _in_proj(x, mod, w_a, w_g, w_r, w_gate, b_gate):
    bsz, s, d = x.shape
    tm = min(INPROJ_ROWS, s)
    assert s % tm == 0 and tm % min(tm, ROW_TILE) == 0
    row = lambda b, i: (b, i, 0)
    const = lambda b, i: (0, 0)
    out = lambda w, dt: (pl.BlockSpec((None, tm, w), row), jax.ShapeDtypeStruct((bsz, s, w), dt))
    outs = [out(ATTN_WIDTH, BF16)] * 3 + [out(GLA_K_WIDTH, F32)] * 2 + [out(GLA_V_WIDTH, BF16)] * 2 \
        + [out(GLA_K_WIDTH, F32)]
    return pl.pallas_call(
        _inproj_kernel,
        grid=(bsz, s // tm),
        in_specs=[
            pl.BlockSpec((None, tm, d), row),
            pl.BlockSpec((None, 6, d), lambda b, i: (b, 0, 0)),
            pl.BlockSpec(w_a.shape, const),
            pl.BlockSpec(w_g.shape, const),
            pl.BlockSpec(w_r.shape, const),
            pl.BlockSpec(w_gate.shape, const),
            pl.BlockSpec(b_gate.shape, const),
        ],
        out_specs=[o[0] for o in outs],
        out_shape=[o[1] for o in outs],
        compiler_params=_params("parallel", "parallel"),
        name="in_proj",
    )(x, mod, w_a, w_g, w_r, w_gate, b_gate)


def _fold_lanes(x, op):
    parts = [x[:, t * LANES:(t + 1) * LANES] for t in range(x.shape[1] // LANES)]
    while len(parts) > 1:
        parts = [op(parts[t], parts[t + 1]) for t in range(0, len(parts), 2)]
    return parts[0]


def _moba_kernel(q_ref, k_ref, v_ref, g_ref, o_ref, km_ref, kidt_ref, vone_ref, qown_ref, qpast_ref,
                 s_ref, sda_ref, sdb_ref, mrun_ref, acc_ref, *, nb):
    blk = MOBA_BLOCK
    tq = 2 * blk
    half = LANES // 2
    nbp = km_ref.shape[0]
    nt = q_ref.shape[0] // tq
    head0 = lax.broadcasted_iota(jnp.int32, (tq, LANES), 1) < half

    def prepare_queries(t):
        q2 = q_ref[pl.ds(pl.multiple_of(t * tq, tq), tq), :].astype(F32)
        zero = jnp.zeros_like(q2)
        q_own = (jnp.where(head0, q2, zero).astype(BF16), jnp.where(head0, zero, q2).astype(BF16))
        km_hi, km_lo = _split_bf16(km_ref[...])
        blk_id = lax.broadcasted_iota(jnp.int32, (nbp, tq), 0).astype(F32)
        second = jnp.where(lax.broadcasted_iota(jnp.int32, (nbp, tq), 1) >= blk, 1.0, 0.0)
        past = blk_id < (2 * t).astype(F32) + second

        def choose(qm):
            cur = jnp.where(past, _dot_nt(km_hi, qm) + _dot_nt(km_lo, qm), NEG)
            sel = jnp.zeros_like(cur)
            for _ in range(MOBA_TOPK):
                mx = jnp.max(cur, axis=0, keepdims=True)
                first = jnp.min(jnp.where(cur == mx, blk_id, float(nbp)), axis=0, keepdims=True)
                hit = blk_id == first
                sel = jnp.where(hit & past, 1.0, sel)
                cur = jnp.where(hit, -jnp.inf, cur)
            return jnp.where(sel > 0.0, 0.0, NEG)

        fill = jnp.zeros((half - nbp, tq), F32)
        bias = jnp.concatenate([choose(q_own[1]), fill, choose(q_own[0]), fill], axis=0).T
        qown_ref[0] = q_own[0]
        qown_ref[1] = q_own[1]
        qpast_ref[0] = jnp.where(head0, q2, bias).astype(BF16)
        qpast_ref[1] = jnp.where(head0, bias, q2).astype(BF16)

    km_ref[...] = jnp.zeros_like(km_ref)
    lane_k = lax.broadcasted_iota(jnp.int32, (blk, LANES), 1)
    h0 = lane_k < half
    one = jnp.ones((blk, LANES), F32)
    zero_k = jnp.zeros((blk, LANES), F32)
    for j in range(nb):
        at = pl.ds(j * blk, blk)
        kb = k_ref[at, :].astype(F32)
        vb = v_ref[at, :]
        km_ref[pl.ds(j, 1), :] = jnp.sum(kb, axis=0, keepdims=True) * (1.0 / blk)
        cols_j = pl.ds((j % 2) * blk, blk)
        kidt_ref[0, j // 2, :, cols_j] = jnp.where(
            h0, kb, jnp.where(lane_k == half + j, one, zero_k)).T.astype(BF16)
        kidt_ref[1, j // 2, :, cols_j] = jnp.where(
            h0, jnp.where(lane_k == j, one, zero_k), kb).T.astype(BF16)
        vone_ref[0, at, :] = jnp.where(h0, vb, one.astype(BF16))
        vone_ref[1, at, :] = jnp.where(h0, one.astype(BF16), vb)
    prepare_queries(jnp.int32(0))

    rows = lax.broadcasted_iota(jnp.int32, (blk, blk), 0)
    cols = lax.broadcasted_iota(jnp.int32, (blk, blk), 1)
    causal = cols <= rows
    halves = (pl.ds(0, blk), pl.ds(blk, blk))
    same_head = ((lax.broadcasted_iota(jnp.int32, (LANES, LANES), 0) < half)
                 == (lax.broadcasted_iota(jnp.int32, (LANES, LANES), 1) < half))
    avg = jnp.where(same_head, 1.0 / HEAD_DIM, 0.0).astype(BF16)

    def finish_tile(t):
        acc = (acc_ref[0], acc_ref[1])
        row_sum = pltpu.roll(jnp.where(head0, acc[1], acc[0]), half, 1)
        o = jnp.where(head0, acc[0], acc[1]) / row_sum
        sq_hi, sq_lo = _split_bf16(o * o)
        ms = _dot(sq_hi, avg) + _dot(sq_lo, avg)
        o_ref[pl.ds(pl.multiple_of(t * tq, tq), tq), :] = (
            o * lax.rsqrt(ms + EPS) * g_ref[...]).astype(BF16)

    acc_ref[...] = jnp.ones_like(acc_ref)

    def query_tile(t, carry):
        finish_tile(jnp.maximum(t - 1, 0))
        q_own = (qown_ref[0], qown_ref[1])
        q_past = (qpast_ref[0], qpast_ref[1])

        row_a = pl.ds(pl.multiple_of(t * tq, tq), blk)
        own = pl.ds(pl.multiple_of(t * tq, tq), tq)
        for h in range(2):
            kt = kidt_ref[h, t]
            s_a = jnp.where(causal, _dot(q_own[h][:blk], kt[:, :blk]), NEG)
            s_b = jnp.concatenate([_dot(q_past[h][blk:], kt[:, :blk]),
                                   jnp.where(causal, _dot(q_own[h][blk:], kt[:, blk:]), NEG)], axis=1)
            sda_ref[h] = s_a
            sdb_ref[h] = s_b
            mrun_ref[h, halves[0], :] = _fold_lanes(s_a, jnp.maximum)
            mrun_ref[h, halves[1], :] = _fold_lanes(s_b, jnp.maximum)

        def for_each_key_group(body):
            def four(j4, c):
                body(4 * j4, 4)
                return c

            lax.fori_loop(0, lax.shift_right_logical(t, 2), four, 0)
            for rest in (1, 2, 3):
                pl.when((t & 3) == rest)(functools.partial(body, t & ~3, rest))

        def score_group(jj0, n):
            for h in range(2):
                best = None
                for i in range(n):
                    s = _dot(q_past[h], kidt_ref[h, jj0 + i])
                    s_ref[h, jj0 + i] = s
                    fold = _fold_lanes(s, jnp.maximum)
                    best = fold if best is None else jnp.maximum(best, fold)
                mrun_ref[h] = jnp.maximum(mrun_ref[h], best)

        for_each_key_group(score_group)

        prepare_queries(jnp.minimum(t + 1, nt - 1))

        m = [[jnp.max(mrun_ref[h, r, :], axis=-1, keepdims=True) for r in halves] for h in range(2)]
        for h in range(2):
            p_a = jnp.exp2(sda_ref[h] - m[h][0])
            p_b = jnp.exp2(sdb_ref[h] - m[h][1])
            acc_ref[h, halves[0], :] = _dot(p_a.astype(BF16), vone_ref[h, row_a, :])
            acc_ref[h, halves[1], :] = _dot(p_b.astype(BF16), vone_ref[h, own, :])

        def pv_group(jj0, n):
            at = pl.ds(pl.multiple_of(jj0 * tq, tq), n * tq)
            for h in range(2):
                p = jnp.concatenate([
                    jnp.concatenate([jnp.exp2(s_ref[h, jj0 + i, r, :] - m[h][e])
                                     for e, r in enumerate(halves)], axis=0).astype(BF16)
                    for i in range(n)], axis=1)
                acc_ref[h] += _dot(p, vone_ref[h, at, :])

        for_each_key_group(pv_group)
        return carry

    lax.fori_loop(0, nt, query_tile, 0)
    finish_tile(nt - 1)


def _moba(qa, ka, va, g_pairs):
    bsz, s, w = qa.shape
    blk = MOBA_BLOCK
    tq = 2 * blk
    assert s % tq == 0 and w % LANES == 0
    nb = s // blk
    nbp = -(-nb // 8) * 8
    assert nbp <= LANES // 2
    npairs = w // LANES
    seq = pl.BlockSpec((None, s, LANES), lambda b, p: (b, 0, p))
    return pl.pallas_call(
        functools.partial(_moba_kernel, nb=nb),
        grid=(bsz, npairs),
        in_specs=[seq, seq, seq, pl.BlockSpec((None, 1, LANES), lambda b, p: (p, 0, 0))],
        out_specs=seq,
        out_shape=jax.ShapeDtypeStruct((bsz, s, w), BF16),
        scratch_shapes=[
            pltpu.VMEM((nbp, LANES), F32),
            pltpu.VMEM((2, s // tq, LANES, tq), BF16),
            pltpu.VMEM((2, s, LANES), BF16),
            pltpu.VMEM((2, tq, LANES), BF16),
            pltpu.VMEM((2, tq, LANES), BF16),
            pltpu.VMEM((2, max(s // tq - 1, 1), tq, tq), F32),
            pltpu.VMEM((2, blk, blk), F32),
            pltpu.VMEM((2, blk, tq), F32),
            pltpu.VMEM((2, tq, LANES), F32),
            pltpu.VMEM((2, tq, LANES), F32),
        ],
        compiler_params=_params("parallel", "parallel"),
        name="moba",
    )(qa, ka, va, g_pairs)


def _gla_kernel(q_ref, k_ref, la_ref, v_ref, gg_ref, g_ref, o_ref):
    c = GLA_CHUNK
    dv = GLA_DV
    half = LANES // 2
    tr = min(GLA_ROWS, q_ref.shape[0])
    shift = c.bit_length() - 1
    grp = min(tr, GLA_SUM_ROWS)
    rows = lax.broadcasted_iota(jnp.int32, (grp, grp), 0)
    cols = lax.broadcasted_iota(jnp.int32, (grp, grp), 1)
    same_chunk = lax.shift_right_logical(rows, shift) == lax.shift_right_logical(cols, shift)
    upto = jnp.where(same_chunk & (cols <= rows), 1.0, 0.0).astype(BF16)
    head0 = lax.broadcasted_iota(jnp.int32, (tr, LANES), 1) < half
    in_chunk_row = lax.broadcasted_iota(jnp.int32, (2 * c, c), 0) & (c - 1)
    causal = lax.broadcasted_iota(jnp.int32, (2 * c, c), 1) <= in_chunk_row

    def row_tile(t, st):
        r0 = pl.multiple_of(t * tr, tr)
        at_tile = pl.ds(r0, tr)
        b = []
        for g0 in range(0, tr, grp):
            la_hi, la_lo = _split_bf16(la_ref[pl.ds(r0 + g0, grp), :])
            b.append(_dot(upto, la_hi) + _dot(upto, la_lo))
        b = jnp.concatenate(b, axis=0)
        b_end = jnp.broadcast_to(b.reshape(tr // c, c, LANES)[:, c - 1:c, :],
                                 (tr // c, c, LANES)).reshape(tr, LANES)
        kk = k_ref[at_tile, :]
        q_dec = q_ref[at_tile, :] * jnp.exp(b)
        k_dec = (kk * jnp.exp(-b)).astype(BF16)
        k_end = (kk * jnp.exp(b_end - b)).astype(BF16)
        decay = jnp.exp(b_end)
        zero = jnp.zeros_like(q_dec)
        q_h0 = jnp.where(head0, q_dec, zero).astype(BF16)
        q_h1 = jnp.where(head0, zero, q_dec).astype(BF16)

        outs = []
        for n in range(tr // c):
            at = slice(n * c, (n + 1) * c)
            qs = jnp.concatenate([q_h0[at], q_h1[at]], axis=0)
            attn = jnp.where(causal, _dot_nt(qs, k_dec[at]), 0.0)
            v_c = v_ref[pl.ds(r0 + n * c, c), :]
            res = _dot(attn.astype(BF16), v_c) + _dot_nt(qs, st.astype(BF16))
            outs.append(jnp.concatenate([res[:c, :dv], res[c:, dv:]], axis=1))
            st = st * decay[n * c:n * c + 1, :] + _dot_tn(v_c, k_end[at])
        o = jnp.concatenate(outs, axis=0)

        for h in range(2):
            head = slice(h * dv, (h + 1) * dv)
            o_h = o[:, head]
            ms = jnp.mean(o_h * o_h, axis=-1, keepdims=True)
            gate = gg_ref[at_tile, head].astype(F32)
            y = o_h * lax.rsqrt(ms + EPS) * g_ref[:, head] * jax.nn.silu(gate)
            o_ref[at_tile, head] = y.astype(BF16)
        return st

    lax.fori_loop(0, q_ref.shape[0] // tr, row_tile, jnp.zeros((2 * dv, LANES), F32))


def _gla(qg, kg, la, vg, gg, g_pairs):
    bsz, s, kw = qg.shape
    vw = vg.shape[-1]
    tr = min(GLA_ROWS, s)
    assert s % tr == 0 and tr % min(tr, GLA_SUM_ROWS) == 0 and tr % GLA_CHUNK == 0
    assert GLA_CHUNK & (GLA_CHUNK - 1) == 0
    npairs = kw // LANES
    kspec = pl.BlockSpec((None, s, LANES), lambda b, p: (b, 0, p))
    vspec = pl.BlockSpec((None, s, 2 * GLA_DV), lambda b, p: (b, 0, p))
    return pl.pallas_call(
        _gla_kernel,
        grid=(bsz, npairs),
        in_specs=[kspec, kspec, kspec, vspec, vspec,
                  pl.BlockSpec((None, 1, 2 * GLA_DV), lambda b, p: (p, 0, 0))],
        out_specs=vspec,
        out_shape=jax.ShapeDtypeStruct((bsz, s, vw), BF16),
        compiler_params=_params("parallel", "parallel"),
        name="gla",
    )(qg, kg, la, vg, gg, g_pairs)


def _mix_ffn_kernel(oa_ref, oah_ref, og_ref, ogh_ref, x_ref, xh_ref, mod_ref, woa_ref, wog_ref,
                    g1_ref, b1_ref, wup_ref, bup_ref, cw_ref, cb_ref, wd_ref, g2_ref, b2_ref,
                    o_ref, x1_ref, a_ref):
    tm = x_ref.shape[0]
    halo = xh_ref.shape[0]
    d_ff = wd_ref.shape[0]
    mod = mod_ref[...]
    oa = jnp.concatenate([oah_ref[...], oa_ref[...]], axis=0)
    og = jnp.concatenate([ogh_ref[...], og_ref[...]], axis=0)
    xc = jnp.concatenate([xh_ref[...], x_ref[...]], axis=0)
    y = _dot(oa, woa_ref[...]) + _dot(og, wog_ref[...])
    x1 = _layer_norm(ALPHA * xc + mod[2:3] * y) * g1_ref[...] + b1_ref[...]
    ucat = (_layer_norm(x1) * (1.0 + mod[4:5]) + mod[3:4]).astype(BF16)
    x1_ref[...] = x1[halo:]
    u = ucat[halo:]
    rowid = lax.broadcasted_iota(jnp.int32, (halo + tm, 1), 0)
    keep = (rowid >= halo) | (pl.program_id(1) > 0)
    for c0 in range(0, d_ff, FF_CHUNK):
        val = slice(c0, c0 + FF_CHUNK)
        gate = slice(d_ff + c0, d_ff + c0 + FF_CHUNK)
        hv = _dot(u, wup_ref[:, val]) + bup_ref[:, val]
        hg = jnp.where(keep, _dot(ucat, wup_ref[:, gate]) + bup_ref[:, gate], 0.0)
        conv = cb_ref[:, val]
        for tap in range(CONV_WIDTH):
            back = CONV_WIDTH - 1 - tap
            conv = conv + cw_ref[tap:tap + 1, val] * hg[halo - back:halo - back + tm]
        a_ref[:, val] = (0.5 * conv * (1.0 + lax.erf(conv * (0.5 ** 0.5))) * hv).astype(BF16)
    h2 = ALPHA * x1_ref[...] + mod[5:6] * _dot(a_ref[...], wd_ref[...])
    o_ref[...] = _layer_norm(h2) * g2_ref[...] + b2_ref[...]


def _mix_ffn(oa, og, x, mod, w_oa, w_og, ln1_g, ln1_b, w_up, b_up, conv_w, conv_b, w_d, ln2_g, ln2_b):
    bsz, s, d = x.shape
    d_ff = w_d.shape[0]
    tm = min(FFN_ROWS, s)
    halo = CONV_HALO
    assert s % tm == 0 and tm % halo == 0 and halo >= CONV_WIDTH - 1 and d_ff % FF_CHUNK == 0
    row = lambda b, i: (b, i, 0)
    prev = lambda b, i: (b, jnp.maximum(i * (tm // halo) - 1, 0), 0)
    const = lambda b, i: (0, 0)
    tile = lambda a: [pl.BlockSpec((None, tm, a.shape[-1]), row),
                      pl.BlockSpec((None, halo, a.shape[-1]), prev)]
    consts = (w_oa, w_og, ln1_g, ln1_b, w_up, b_up, conv_w, conv_b, w_d, ln2_g, ln2_b)
    return pl.pallas_call(
        _mix_ffn_kernel,
        grid=(bsz, s // tm),
        in_specs=tile(oa) + tile(og) + tile(x) + [pl.BlockSpec((None, 6, d), lambda b, i: (b, 0, 0))]
        + [pl.BlockSpec(a.shape, const) for a in consts],
        out_specs=pl.BlockSpec((None, tm, d), row),
        out_shape=jax.ShapeDtypeStruct((bsz, s, d), F32),
        scratch_shapes=[pltpu.VMEM((tm, d), F32), pltpu.VMEM((tm, d_ff), BF16)],
        compiler_params=_params("parallel", "parallel"),
        name="mix_ffn",
    )(oa, oa, og, og, x, x, mod, *consts)


def _layer(x, mod, w_in, w_gla_gate, b_gla_gate, attn_norm_g, gla_norm_g, w_o, ln1_g, ln1_b,
           w_up, b_up, conv_w, conv_b, w_down, ln2_g, ln2_b):
    d = x.shape[-1]
    d_ff = w_down.shape[0]
    a3 = 3 * ATTN_WIDTH
    g_end = a3 + 2 * GLA_K_WIDTH + 2 * GLA_V_WIDTH
    w_a = w_in[:, :a3].astype(BF16)
    w_g = w_in[:, a3:g_end].astype(BF16)
    w_r = jnp.pad(w_in[:, g_end:], ((0, 0), (0, LANES - GLA_GATE_RANK))).astype(BF16)
    w_gate = jnp.pad(w_gla_gate, ((0, LANES - GLA_GATE_RANK), (0, 0)))
    qa, ka, va, qg, kg, vg, gg, la = _in_proj(x, mod, w_a, w_g, w_r, w_gate,
                                              b_gla_gate.reshape(1, -1))
    oa = _moba(qa, ka, va, attn_norm_g.reshape(-1, 1, LANES))
    og = _gla(qg, kg, la, vg, gg, gla_norm_g.reshape(-1, 1, 2 * GLA_DV))
    w_ob = w_o.astype(BF16)
    return _mix_ffn(oa, og, x, mod, w_ob[:ATTN_WIDTH], w_ob[ATTN_WIDTH:],
                    ln1_g.reshape(1, d), ln1_b.reshape(1, d), w_up.astype(BF16), b_up.reshape(1, -1),
                    conv_w, conv_b.reshape(1, -1), w_down.astype(BF16),
                    ln2_g.reshape(1, d), ln2_b.reshape(1, d))


def kernel(x, c, w_ada, b_ada, w_in, w_gla_gate, b_gla_gate, attn_norm_g, gla_norm_g, w_o,
           ln1_g, ln1_b, w_up, b_up, conv_w, conv_b, w_down, ln2_g, ln2_b):
    bsz, _, d = x.shape
    for l in range(w_in.shape[0]):
        mod = _adaln_mod(c, w_ada[l], b_ada[l]).reshape(bsz, 6, d)
        x = _layer(x, mod, w_in[l], w_gla_gate[l], b_gla_gate[l], attn_norm_g[l], gla_norm_g[l],
                   w_o[l], ln1_g[l], ln1_b[l], w_up[l], b_up[l], conv_w[l], conv_b[l], w_down[l],
                   ln2_g[l], ln2_b[l])
    return x
```

```python
import functools

import jax
import jax.numpy as jnp
from jax import lax
from jax.experimental import pallas as pl
from jax.experimental.pallas import tpu as pltpu

F32 = jnp.float32
BF16 = jnp.bfloat16

HEAD_DIM = 64
N_ATTN_HEADS = 8
ATTN_WIDTH = N_ATTN_HEADS * HEAD_DIM
N_GLA_HEADS = 4
GLA_DK = 64
GLA_DV = 128
GLA_K_WIDTH = N_GLA_HEADS * GLA_DK
GLA_V_WIDTH = N_GLA_HEADS * GLA_DV
GLA_GATE_RANK = 16
GLA_TAU = 16.0
GLA_CHUNK = 64
MOBA_BLOCK = 256
MOBA_TOPK = 3
CONV_WIDTH = 3
DEPTH = 1
ALPHA = (2.0 * DEPTH) ** 0.25
EPS = 1e-5
NEG = -1e30
LOG2E = 1.4426950408889634

LANES = 128
BF16_ROWS = 16
VMEM_LIMIT = 56 * 1024 * 1024

ROW_TILE = 512
FFN_ROWS = 1024
INPROJ_ROWS = 1024
GLA_ROWS = 2048
GLA_SUM_ROWS = 256
FF_CHUNK = 256
CONV_HALO = BF16_ROWS


def _dot(a, b):
    return jnp.dot(a, b, preferred_element_type=F32)


def _dot_nt(a, b):
    return lax.dot_general(a, b, (((1,), (1,)), ((), ())), preferred_element_type=F32)


def _dot_tn(a, b):
    return lax.dot_general(a, b, (((0,), (0,)), ((), ())), preferred_element_type=F32)


def _split_bf16(a):
    hi = a.astype(BF16)
    lo = (a - hi.astype(F32)).astype(BF16)
    return hi, lo


def _dot3(a, b):
    ah, al = _split_bf16(a)
    bh, bl = _split_bf16(b)
    return _dot(ah, bh) + (_dot(ah, bl) + _dot(al, bh))


def _layer_norm(x):
    mu = jnp.mean(x, -1, keepdims=True)
    xc = x - mu
    var = jnp.mean(xc * xc, -1, keepdims=True)
    return xc * lax.rsqrt(var + EPS)


def _params(*sem):
    return pltpu.CompilerParams(dimension_semantics=sem, vmem_limit_bytes=VMEM_LIMIT)


def _mod_kernel(c_ref, w_ref, b_ref, o_ref):
    o_ref[...] = _dot3(jax.nn.silu(c_ref[...]), w_ref[...]) + b_ref[...]


def _adaln_mod(c, w, b):
    bsz, d = c.shape
    n = w.shape[1]
    return pl.pallas_call(
        _mod_kernel,
        grid=(n // d,),
        in_specs=[
            pl.BlockSpec((bsz, d), lambda j: (0, 0)),
            pl.BlockSpec((d, d), lambda j: (0, j)),
            pl.BlockSpec((1, d), lambda j: (0, j)),
        ],
        out_specs=pl.BlockSpec((bsz, d), lambda j: (0, j)),
        out_shape=jax.ShapeDtypeStruct((bsz, n), F32),
        compiler_params=_params("parallel"),
        name="adaln_mod",
    )(c, w, b.reshape(1, n))


def _inproj_kernel(x_ref, mod_ref, wa_ref, wg_ref, wr_ref, wgate_ref, bgate_ref,
                   qa_ref, ka_ref, va_ref, qg_ref, kg_ref, vg_ref, gg_ref, la_ref):
    mod = mod_ref[...]
    aw, kw, vw = ATTN_WIDTH, GLA_K_WIDTH, GLA_V_WIDTH
    sub = min(x_ref.shape[0], ROW_TILE)
    tiles = [pl.ds(r0, sub) for r0 in range(0, x_ref.shape[0], sub)]
    us = [(_layer_norm(x_ref[at, :]) * (1.0 + mod[1:2]) + mod[0:1]).astype(BF16) for at in tiles]
    for at, u in zip(tiles, us):
        pa = _dot(u, wa_ref[...])
        qa_ref[at, :] = (pa[:, :aw] * (HEAD_DIM ** -0.5 * LOG2E)).astype(BF16)
        ka_ref[at, :] = pa[:, aw:2 * aw].astype(BF16)
        va_ref[at, :] = pa[:, 2 * aw:].astype(BF16)
        pg = _dot(u, wg_ref[...])
        qg_ref[at, :] = pg[:, :kw] * GLA_DK ** -0.5
        kg_ref[at, :] = pg[:, kw:2 * kw]
        vg_ref[at, :] = pg[:, 2 * kw:2 * kw + vw].astype(BF16)
        gg_ref[at, :] = pg[:, 2 * kw + vw:].astype(BF16)
        z = _dot3(_dot(u, wr_ref[...]), wgate_ref[...]) + bgate_ref[...]
        la_ref[at, :] = (jnp.minimum(z, 0.0) - jnp.log1p(jnp.exp(-jnp.abs(z)))) * (1.0 / GLA_TAU)


def _in_proj(x, mod, w_a, w_g, w_r, w_gate, b_gate):
    bsz, s, d = x.shape
    tm = min(INPROJ_ROWS, s)
    assert s % tm == 0 and tm % min(tm, ROW_TILE) == 0
    row = lambda b, i: (b, i, 0)
    const = lambda b, i: (0, 0)
    out = lambda w, dt: (pl.BlockSpec((None, tm, w), row), jax.ShapeDtypeStruct((bsz, s, w), dt))
    outs = [out(ATTN_WIDTH, BF16)] * 3 + [out(GLA_K_WIDTH, F32)] * 2 + [out(GLA_V_WIDTH, BF16)] * 2 \
        + [out(GLA_K_WIDTH, F32)]
    return pl.pallas_call(
        _inproj_kernel,
        grid=(bsz, s // tm),
        in_specs=[
            pl.BlockSpec((None, tm, d), row),
            pl.BlockSpec((None, 6, d), lambda b, i: (b, 0, 0)),
            pl.BlockSpec(w_a.shape, const),
            pl.BlockSpec(w_g.shape, const),
            pl.BlockSpec(w_r.shape, const),
            pl.BlockSpec(w_gate.shape, const),
            pl.BlockSpec(b_gate.shape, const),
        ],
        out_specs=[o[0] for o in outs],
        out_shape=[o[1] for o in outs],
        compiler_params=_params("parallel", "parallel"),
        name="in_proj",
    )(x, mod, w_a, w_g, w_r, w_gate, b_gate)


def _fold_lanes(x, op):
    parts = [x[:, t * LANES:(t + 1) * LANES] for t in range(x.shape[1] // LANES)]
    while len(parts) > 1:
        parts = [op(parts[t], parts[t + 1]) for t in range(0, len(parts), 2)]
    return parts[0]


def _moba_kernel(q_ref, k_ref, v_ref, g_ref, o_ref, km_ref, kidt_ref, vone_ref, qown_ref, qpast_ref,
                 s_ref, sda_ref, sdb_ref, mrun_ref, acc_ref, *, nb):
    blk = MOBA_BLOCK
    tq = 2 * blk
    half = LANES // 2
    nbp = km_ref.shape[0]
    nt = q_ref.shape[0] // tq
    head0 = lax.broadcasted_iota(jnp.int32, (tq, LANES), 1) < half

    def prepare_queries(t):
        q2 = q_ref[pl.ds(t * tq, tq), :].astype(F32)
        zero = jnp.zeros_like(q2)
        q_own = (jnp.where(head0, q2, zero).astype(BF16), jnp.where(head0, zero, q2).astype(BF16))
        km_hi, km_lo = _split_bf16(km_ref[...])
        blk_id = lax.broadcasted_iota(jnp.int32, (nbp, tq), 0).astype(F32)
        second = jnp.where(lax.broadcasted_iota(jnp.int32, (nbp, tq), 1) >= blk, 1.0, 0.0)
        past = blk_id < float(2 * t) + second

        def choose(qm):
            cur = jnp.where(past, _dot_nt(km_hi, qm) + _dot_nt(km_lo, qm), NEG)
            sel = jnp.zeros_like(cur)
            for _ in range(MOBA_TOPK):
                mx = jnp.max(cur, axis=0, keepdims=True)
                first = jnp.min(jnp.where(cur == mx, blk_id, float(nbp)), axis=0, keepdims=True)
                hit = blk_id == first
                sel = jnp.where(hit & past, 1.0, sel)
                cur = jnp.where(hit, -jnp.inf, cur)
            return jnp.where(sel > 0.0, 0.0, NEG)

        fill = jnp.zeros((half - nbp, tq), F32)
        bias = jnp.concatenate([choose(q_own[1]), fill, choose(q_own[0]), fill], axis=0).T
        qown_ref[0] = q_own[0]
        qown_ref[1] = q_own[1]
        qpast_ref[0] = jnp.where(head0, q2, bias).astype(BF16)
        qpast_ref[1] = jnp.where(head0, bias, q2).astype(BF16)

    km_ref[...] = jnp.zeros_like(km_ref)
    lane_k = lax.broadcasted_iota(jnp.int32, (blk, LANES), 1)
    h0 = lane_k < half
    one = jnp.ones((blk, LANES), F32)
    zero_k = jnp.zeros((blk, LANES), F32)
    for j in range(nb):
        at = pl.ds(j * blk, blk)
        kb = k_ref[at, :].astype(F32)
        vb = v_ref[at, :]
        km_ref[pl.ds(j, 1), :] = jnp.sum(kb, axis=0, keepdims=True) * (1.0 / blk)
        cols_j = pl.ds((j % 2) * blk, blk)
        kidt_ref[0, j // 2, :, cols_j] = jnp.where(
            h0, kb, jnp.where(lane_k == half + j, one, zero_k)).T.astype(BF16)
        kidt_ref[1, j // 2, :, cols_j] = jnp.where(
            h0, jnp.where(lane_k == j, one, zero_k), kb).T.astype(BF16)
        vone_ref[0, at, :] = jnp.where(h0, vb, one.astype(BF16))
        vone_ref[1, at, :] = jnp.where(h0, one.astype(BF16), vb)

    rows = lax.broadcasted_iota(jnp.int32, (blk, blk), 0)
    cols = lax.broadcasted_iota(jnp.int32, (blk, blk), 1)
    causal = cols <= rows
    halves = (pl.ds(0, blk), pl.ds(blk, blk))
    same_head = ((lax.broadcasted_iota(jnp.int32, (LANES, LANES), 0) < half)
                 == (lax.broadcasted_iota(jnp.int32, (LANES, LANES), 1) < half))
    avg = jnp.where(same_head, 1.0 / HEAD_DIM, 0.0).astype(BF16)

    def finish_tile(t):
        acc = (acc_ref[0], acc_ref[1])
        row_sum = pltpu.roll(jnp.where(head0, acc[1], acc[0]), half, 1)
        o = jnp.where(head0, acc[0], acc[1]) / row_sum
        sq_hi, sq_lo = _split_bf16(o * o)
        ms = _dot(sq_hi, avg) + _dot(sq_lo, avg)
        o_ref[pl.ds(t * tq, tq), :] = (o * lax.rsqrt(ms + EPS) * g_ref[...]).astype(BF16)

    def query_tile(t):
        q_own = (qown_ref[0], qown_ref[1])
        q_past = (qpast_ref[0], qpast_ref[1])

        row_a = pl.ds(t * tq, blk)
        own = pl.ds(t * tq, tq)
        for h in range(2):
            kt = kidt_ref[h, t]
            s_a = jnp.where(causal, _dot(q_own[h][:blk], kt[:, :blk]), NEG)
            s_b = jnp.concatenate([_dot(q_past[h][blk:], kt[:, :blk]),
                                   jnp.where(causal, _dot(q_own[h][blk:], kt[:, blk:]), NEG)], axis=1)
            sda_ref[h] = s_a
            sdb_ref[h] = s_b
            mrun_ref[h, halves[0], :] = _fold_lanes(s_a, jnp.maximum)
            mrun_ref[h, halves[1], :] = _fold_lanes(s_b, jnp.maximum)

        for h in range(2):
            best = mrun_ref[h]
            for jj in range(t):
                s = _dot(q_past[h], kidt_ref[h, jj])
                s_ref[h, jj] = s
                best = jnp.maximum(best, _fold_lanes(s, jnp.maximum))
            mrun_ref[h] = best

        m = [[jnp.max(mrun_ref[h, r, :], axis=-1, keepdims=True) for r in halves] for h in range(2)]
        for h in range(2):
            p_a = jnp.exp2(sda_ref[h] - m[h][0])
            p_b = jnp.exp2(sdb_ref[h] - m[h][1])
            acc_ref[h, halves[0], :] = _dot(p_a.astype(BF16), vone_ref[h, row_a, :])
            acc_ref[h, halves[1], :] = _dot(p_b.astype(BF16), vone_ref[h, own, :])

        for h in range(2):
            if t:
                p = jnp.concatenate([
                    jnp.concatenate([jnp.exp2(s_ref[h, jj, r, :] - m[h][e])
                                     for e, r in enumerate(halves)], axis=0).astype(BF16)
                    for jj in range(t)], axis=1)
                acc_ref[h] += _dot(p, vone_ref[h, pl.ds(0, t * tq), :])

    for t in range(nt):
        prepare_queries(t)
        query_tile(t)
        finish_tile(t)


def _moba(qa, ka, va, g_pairs):
    bsz, s, w = qa.shape
    blk = MOBA_BLOCK
    tq = 2 * blk
    assert s % tq == 0 and w % LANES == 0
    nb = s // blk
    nbp = -(-nb // 8) * 8
    assert nbp <= LANES // 2
    npairs = w // LANES
    seq = pl.BlockSpec((None, s, LANES), lambda b, p: (b, 0, p))
    return pl.pallas_call(
        functools.partial(_moba_kernel, nb=nb),
        grid=(bsz, npairs),
        in_specs=[seq, seq, seq, pl.BlockSpec((None, 1, LANES), lambda b, p: (p, 0, 0))],
        out_specs=seq,
        out_shape=jax.ShapeDtypeStruct((bsz, s, w), BF16),
        scratch_shapes=[
            pltpu.VMEM((nbp, LANES), F32),
            pltpu.VMEM((2, s // tq, LANES, tq), BF16),
            pltpu.VMEM((2, s, LANES), BF16),
            pltpu.VMEM((2, tq, LANES), BF16),
            pltpu.VMEM((2, tq, LANES), BF16),
            pltpu.VMEM((2, max(s // tq - 1, 1), tq, tq), F32),
            pltpu.VMEM((2, blk, blk), F32),
            pltpu.VMEM((2, blk, tq), F32),
            pltpu.VMEM((2, tq, LANES), F32),
            pltpu.VMEM((2, tq, LANES), F32),
        ],
        compiler_params=_params("parallel", "parallel"),
        name="moba",
    )(qa, ka, va, g_pairs)


def _gla_kernel(q_ref, k_ref, la_ref, v_ref, gg_ref, g_ref, o_ref):
    c = GLA_CHUNK
    dv = GLA_DV
    half = LANES // 2
    tr = min(GLA_ROWS, q_ref.shape[0])
    shift = c.bit_length() - 1
    grp = min(tr, GLA_SUM_ROWS)
    rows = lax.broadcasted_iota(jnp.int32, (grp, grp), 0)
    cols = lax.broadcasted_iota(jnp.int32, (grp, grp), 1)
    same_chunk = lax.shift_right_logical(rows, shift) == lax.shift_right_logical(cols, shift)
    upto = jnp.where(same_chunk & (cols <= rows), 1.0, 0.0).astype(BF16)
    head0 = lax.broadcasted_iota(jnp.int32, (tr, LANES), 1) < half
    in_chunk_row = lax.broadcasted_iota(jnp.int32, (2 * c, c), 0) & (c - 1)
    causal = lax.broadcasted_iota(jnp.int32, (2 * c, c), 1) <= in_chunk_row

    def row_tile(t, st):
        r0 = pl.multiple_of(t * tr, tr)
        at_tile = pl.ds(r0, tr)
        b = []
        for g0 in range(0, tr, grp):
            la_hi, la_lo = _split_bf16(la_ref[pl.ds(r0 + g0, grp), :])
            b.append(_dot(upto, la_hi) + _dot(upto, la_lo))
        b = jnp.concatenate(b, axis=0)
        b_end = jnp.broadcast_to(b.reshape(tr // c, c, LANES)[:, c - 1:c, :],
                                 (tr // c, c, LANES)).reshape(tr, LANES)
        kk = k_ref[at_tile, :]
        q_dec = q_ref[at_tile, :] * jnp.exp(b)
        k_dec = (kk * jnp.exp(-b)).astype(BF16)
        k_end = (kk * jnp.exp(b_end - b)).astype(BF16)
        decay = jnp.exp(b_end)
        zero = jnp.zeros_like(q_dec)
        q_h0 = jnp.where(head0, q_dec, zero).astype(BF16)
        q_h1 = jnp.where(head0, zero, q_dec).astype(BF16)

        outs = []
        for n in range(tr // c):
            at = slice(n * c, (n + 1) * c)
            qs = jnp.concatenate([q_h0[at], q_h1[at]], axis=0)
            attn = jnp.where(causal, _dot_nt(qs, k_dec[at]), 0.0)
            v_c = v_ref[pl.ds(r0 + n * c, c), :]
            res = _dot(attn.astype(BF16), v_c) + _dot_nt(qs, st.astype(BF16))
            outs.append(jnp.concatenate([res[:c, :dv], res[c:, dv:]], axis=1))
            st = st * decay[n * c:n * c + 1, :] + _dot_tn(v_c, k_end[at])
        o = jnp.concatenate(outs, axis=0)

        for h in range(2):
            head = slice(h * dv, (h + 1) * dv)
            o_h = o[:, head]
            ms = jnp.mean(o_h * o_h, axis=-1, keepdims=True)
            gate = gg_ref[at_tile, head].astype(F32)
            y = o_h * lax.rsqrt(ms + EPS) * g_ref[:, head] * jax.nn.silu(gate)
            o_ref[at_tile, head] = y.astype(BF16)
        return st

    lax.fori_loop(0, q_ref.shape[0] // tr, row_tile, jnp.zeros((2 * dv, LANES), F32))


def _gla(qg, kg, la, vg, gg, g_pairs):
    bsz, s, kw = qg.shape
    vw = vg.shape[-1]
    tr = min(GLA_ROWS, s)
    assert s % tr == 0 and tr % min(tr, GLA_SUM_ROWS) == 0 and tr % GLA_CHUNK == 0
    assert GLA_CHUNK & (GLA_CHUNK - 1) == 0
    npairs = kw // LANES
    kspec = pl.BlockSpec((None, s, LANES), lambda b, p: (b, 0, p))
    vspec = pl.BlockSpec((None, s, 2 * GLA_DV), lambda b, p: (b, 0, p))
    return pl.pallas_call(
        _gla_kernel,
        grid=(bsz, npairs),
        in_specs=[kspec, kspec, kspec, vspec, vspec,
                  pl.BlockSpec((None, 1, 2 * GLA_DV), lambda b, p: (p, 0, 0))],
        out_specs=vspec,
        out_shape=jax.ShapeDtypeStruct((bsz, s, vw), BF16),
        compiler_params=_params("parallel", "parallel"),
        name="gla",
    )(qg, kg, la, vg, gg, g_pairs)


def _mix_ffn_kernel(oa_ref, oah_ref, og_ref, ogh_ref, x_ref, xh_ref, mod_ref, woa_ref, wog_ref,
                    g1_ref, b1_ref, wup_ref, bup_ref, cw_ref, cb_ref, wd_ref, g2_ref, b2_ref,
                    o_ref, x1_ref, a_ref):
    tm = x_ref.shape[0]
    halo = xh_ref.shape[0]
    d_ff = wd_ref.shape[0]
    mod = mod_ref[...]
    oa = jnp.concatenate([oah_ref[...], oa_ref[...]], axis=0)
    og = jnp.concatenate([ogh_ref[...], og_ref[...]], axis=0)
    xc = jnp.concatenate([xh_ref[...], x_ref[...]], axis=0)
    y = _dot(oa, woa_ref[...]) + _dot(og, wog_ref[...])
    x1 = _layer_norm(ALPHA * xc + mod[2:3] * y) * g1_ref[...] + b1_ref[...]
    ucat = (_layer_norm(x1) * (1.0 + mod[4:5]) + mod[3:4]).astype(BF16)
    x1_ref[...] = x1[halo:]
    u = ucat[halo:]
    rowid = lax.broadcasted_iota(jnp.int32, (halo + tm, 1), 0)
    keep = (rowid >= halo) | (pl.program_id(1) > 0)
    for c0 in range(0, d_ff, FF_CHUNK):
        val = slice(c0, c0 + FF_CHUNK)
        gate = slice(d_ff + c0, d_ff + c0 + FF_CHUNK)
        hv = _dot(u, wup_ref[:, val]) + bup_ref[:, val]
        hg = jnp.where(keep, _dot(ucat, wup_ref[:, gate]) + bup_ref[:, gate], 0.0)
        conv = cb_ref[:, val]
        for tap in range(CONV_WIDTH):
            back = CONV_WIDTH - 1 - tap
            conv = conv + cw_ref[tap:tap + 1, val] * hg[halo - back:halo - back + tm]
        a_ref[:, val] = (0.5 * conv * (1.0 + lax.erf(conv * (0.5 ** 0.5))) * hv).astype(BF16)
    h2 = ALPHA * x1_ref[...] + mod[5:6] * _dot(a_ref[...], wd_ref[...])
    o_ref[...] = _layer_norm(h2) * g2_ref[...] + b2_ref[...]


def _mix_ffn(oa, og, x, mod, w_oa, w_og, ln1_g, ln1_b, w_up, b_up, conv_w, conv_b, w_d, ln2_g, ln2_b):
    bsz, s, d = x.shape
    d_ff = w_d.shape[0]
    tm = min(FFN_ROWS, s)
    halo = CONV_HALO
    assert s % tm == 0 and tm % halo == 0 and halo >= CONV_WIDTH - 1 and d_ff % FF_CHUNK == 0
    row = lambda b, i: (b, i, 0)
    prev = lambda b, i: (b, jnp.maximum(i * (tm // halo) - 1, 0), 0)
    const = lambda b, i: (0, 0)
    tile = lambda a: [pl.BlockSpec((None, tm, a.shape[-1]), row),
                      pl.BlockSpec((None, halo, a.shape[-1]), prev)]
    consts = (w_oa, w_og, ln1_g, ln1_b, w_up, b_up, conv_w, conv_b, w_d, ln2_g, ln2_b)
    return pl.pallas_call(
        _mix_ffn_kernel,
        grid=(bsz, s // tm),
        in_specs=tile(oa) + tile(og) + tile(x) + [pl.BlockSpec((None, 6, d), lambda b, i: (b, 0, 0))]
        + [pl.BlockSpec(a.shape, const) for a in consts],
        out_specs=pl.BlockSpec((None, tm, d), row),
        out_shape=jax.ShapeDtypeStruct((bsz, s, d), F32),
        scratch_shapes=[pltpu.VMEM((tm, d), F32), pltpu.VMEM((tm, d_ff), BF16)],
        compiler_params=_params("parallel", "parallel"),
        name="mix_ffn",
    )(oa, oa, og, og, x, x, mod, *consts)


def _layer(x, mod, w_in, w_gla_gate, b_gla_gate, attn_norm_g, gla_norm_g, w_o, ln1_g, ln1_b,
           w_up, b_up, conv_w, conv_b, w_down, ln2_g, ln2_b):
    d = x.shape[-1]
    d_ff = w_down.shape[0]
    a3 = 3 * ATTN_WIDTH
    g_end = a3 + 2 * GLA_K_WIDTH + 2 * GLA_V_WIDTH
    w_a = w_in[:, :a3].astype(BF16)
    w_g = w_in[:, a3:g_end].astype(BF16)
    w_r = jnp.pad(w_in[:, g_end:], ((0, 0), (0, LANES - GLA_GATE_RANK))).astype(BF16)
    w_gate = jnp.pad(w_gla_gate, ((0, LANES - GLA_GATE_RANK), (0, 0)))
    qa, ka, va, qg, kg, vg, gg, la = _in_proj(x, mod, w_a, w_g, w_r, w_gate,
                                              b_gla_gate.reshape(1, -1))
    oa = _moba(qa, ka, va, attn_norm_g.reshape(-1, 1, LANES))
    og = _gla(qg, kg, la, vg, gg, gla_norm_g.reshape(-1, 1, 2 * GLA_DV))
    w_ob = w_o.astype(BF16)
    return _mix_ffn(oa, og, x, mod, w_ob[:ATTN_WIDTH], w_ob[ATTN_WIDTH:],
                    ln1_g.reshape(1, d), ln1_b.reshape(1, d), w_up.astype(BF16), b_up.reshape(1, -1),
                    conv_w, conv_b.reshape(1, -1), w_down.astype(BF16),
                    ln2_g.reshape(1, d), ln2_b.reshape(1, d))


def kernel(x, c, w_ada, b_ada, w_in, w_gla_gate, b_gla_gate, attn_norm_g, gla_norm_g, w_o,
           ln1_g, ln1_b, w_up, b_up, conv_w, conv_b, w_down, ln2_g, ln2_b):
    bsz, _, d = x.shape
    for l in range(w_in.shape[0]):
        mod = _adaln_mod(c, w_ada[l], b_ada[l]).reshape(bsz, 6, d)
        x = _layer(x, mod, w_in[l], w_gla_gate[l], b_gla_gate[l], attn_norm_g[l], gla_norm_g[l],
                   w_o[l], ln1_g[l], ln1_b[l], w_up[l], b_up[l], conv_w[l], conv_b[l], w_down[l],
                   ln2_g[l], ln2_b[l])
    return x
```

```python
import functools

import jax
import jax.numpy as jnp
from jax import lax
from jax.experimental import pallas as pl
from jax.experimental.pallas import tpu as pltpu

F32 = jnp.float32
BF16 = jnp.bfloat16

HEAD_DIM = 64
N_ATTN_HEADS = 8
ATTN_WIDTH = N_ATTN_HEADS * HEAD_DIM
N_GLA_HEADS = 4
GLA_DK = 64
GLA_DV = 128
GLA_K_WIDTH = N_GLA_HEADS * GLA_DK
GLA_V_WIDTH = N_GLA_HEADS * GLA_DV
GLA_GATE_RANK = 16
GLA_TAU = 16.0
GLA_CHUNK = 64
MOBA_BLOCK = 256
MOBA_TOPK = 3
CONV_WIDTH = 3
DEPTH = 1
ALPHA = (2.0 * DEPTH) ** 0.25
EPS = 1e-5
NEG = -1e30
LOG2E = 1.4426950408889634

LANES = 128
BF16_ROWS = 16
VMEM_LIMIT = 56 * 1024 * 1024

ROW_TILE = 512
FFN_ROWS = 1024
INPROJ_ROWS = 1024
GLA_ROWS = 2048
GLA_SUM_ROWS = 256
FF_CHUNK = 256
CONV_HALO = BF16_ROWS


def _dot(a, b):
    return jnp.dot(a, b, preferred_element_type=F32)


def _dot_nt(a, b):
    return lax.dot_general(a, b, (((1,), (1,)), ((), ())), preferred_element_type=F32)


def _dot_tn(a, b):
    return lax.dot_general(a, b, (((0,), (0,)), ((), ())), preferred_element_type=F32)


def _split_bf16(a):
    hi = a.astype(BF16)
    lo = (a - hi.astype(F32)).astype(BF16)
    return hi, lo


def _dot3(a, b):
    ah, al = _split_bf16(a)
    bh, bl = _split_bf16(b)
    return _dot(ah, bh) + (_dot(ah, bl) + _dot(al, bh))


def _layer_norm(x):
    mu = jnp.mean(x, -1, keepdims=True)
    xc = x - mu
    var = jnp.mean(xc * xc, -1, keepdims=True)
    return xc * lax.rsqrt(var + EPS)


def _params(*sem):
    return pltpu.CompilerParams(dimension_semantics=sem, vmem_limit_bytes=VMEM_LIMIT)


def _mod_kernel(c_ref, w_ref, b_ref, o_ref):
    o_ref[...] = _dot3(jax.nn.silu(c_ref[...]), w_ref[...]) + b_ref[...]


def _adaln_mod(c, w, b):
    bsz, d = c.shape
    n = w.shape[1]
    return pl.pallas_call(
        _mod_kernel,
        grid=(n // d,),
        in_specs=[
            pl.BlockSpec((bsz, d), lambda j: (0, 0)),
            pl.BlockSpec((d, d), lambda j: (0, j)),
            pl.BlockSpec((1, d), lambda j: (0, j)),
        ],
        out_specs=pl.BlockSpec((bsz, d), lambda j: (0, j)),
        out_shape=jax.ShapeDtypeStruct((bsz, n), F32),
        compiler_params=_params("parallel"),
        name="adaln_mod",
    )(c, w, b.reshape(1, n))


def _inproj_kernel(x_ref, mod_ref, wa_ref, wg_ref, wr_ref, wgate_ref, bgate_ref,
                   qa_ref, ka_ref, va_ref, qg_ref, kg_ref, vg_ref, gg_ref, la_ref):
    mod = mod_ref[...]
    aw, kw, vw = ATTN_WIDTH, GLA_K_WIDTH, GLA_V_WIDTH
    sub = min(x_ref.shape[0], ROW_TILE)
    tiles = [pl.ds(r0, sub) for r0 in range(0, x_ref.shape[0], sub)]
    us = [(_layer_norm(x_ref[at, :]) * (1.0 + mod[1:2]) + mod[0:1]).astype(BF16) for at in tiles]
    for at, u in zip(tiles, us):
        pa = _dot(u, wa_ref[...])
        qa_ref[at, :] = (pa[:, :aw] * (HEAD_DIM ** -0.5 * LOG2E)).astype(BF16)
        ka_ref[at, :] = pa[:, aw:2 * aw].astype(BF16)
        va_ref[at, :] = pa[:, 2 * aw:].astype(BF16)
        pg = _dot(u, wg_ref[...])
        qg_ref[at, :] = pg[:, :kw] * GLA_DK ** -0.5
        kg_ref[at, :] = pg[:, kw:2 * kw]
        vg_ref[at, :] = pg[:, 2 * kw:2 * kw + vw].astype(BF16)
        gg_ref[at, :] = pg[:, 2 * kw + vw:].astype(BF16)
        z = _dot3(_dot(u, wr_ref[...]), wgate_ref[...]) + bgate_ref[...]
        la_ref[at, :] = (jnp.minimum(z, 0.0) - jnp.log1p(jnp.exp(-jnp.abs(z)))) * (1.0 / GLA_TAU)


def _in_proj(x, mod, w_a, w_g, w_r, w_gate, b_gate):
    bsz, s, d = x.shape
    tm = min(INPROJ_ROWS, s)
    assert s % tm == 0 and tm % min(tm, ROW_TILE) == 0
    row = lambda b, i: (b, i, 0)
    const = lambda b, i: (0, 0)
    out = lambda w, dt: (pl.BlockSpec((None, tm, w), row), jax.ShapeDtypeStruct((bsz, s, w), dt))
    outs = [out(ATTN_WIDTH, BF16)] * 3 + [out(GLA_K_WIDTH, F32)] * 2 + [out(GLA_V_WIDTH, BF16)] * 2 \
        + [out(GLA_K_WIDTH, F32)]
    return pl.pallas_call(
        _inproj_kernel,
        grid=(bsz, s // tm),
        in_specs=[
            pl.BlockSpec((None, tm, d), row),
            pl.BlockSpec((None, 6, d), lambda b, i: (b, 0, 0)),
            pl.BlockSpec(w_a.shape, const),
            pl.BlockSpec(w_g.shape, const),
            pl.BlockSpec(w_r.shape, const),
            pl.BlockSpec(w_gate.shape, const),
            pl.BlockSpec(b_gate.shape, const),
        ],
        out_specs=[o[0] for o in outs],
        out_shape=[o[1] for o in outs],
        compiler_params=_params("parallel", "parallel"),
        name="in_proj",
    )(x, mod, w_a, w_g, w_r, w_gate, b_gate)


def _fold_lanes(x, op):
    parts = [x[:, t * LANES:(t + 1) * LANES] for t in range(x.shape[1] // LANES)]
    while len(parts) > 1:
        parts = [op(parts[t], parts[t + 1]) for t in range(0, len(parts), 2)]
    return parts[0]


def _moba_kernel(q_ref, k_ref, v_ref, g_ref, o_ref, km_ref, kidt_ref, vone_ref, qown_ref, qpast_ref,
                 s_ref, sda_ref, sdb_ref, mrun_ref, acc_ref, *, nb):
    blk = MOBA_BLOCK
    tq = 2 * blk
    half = LANES // 2
    nbp = km_ref.shape[0]
    nt = q_ref.shape[0] // tq
    head0 = lax.broadcasted_iota(jnp.int32, (tq, LANES), 1) < half

    def prepare_queries(t):
        q2 = q_ref[pl.ds(t * tq, tq), :].astype(F32)
        zero = jnp.zeros_like(q2)
        q_own = (jnp.where(head0, q2, zero).astype(BF16), jnp.where(head0, zero, q2).astype(BF16))
        km_hi, km_lo = _split_bf16(km_ref[...])
        blk_id = lax.broadcasted_iota(jnp.int32, (nbp, tq), 0).astype(F32)
        second = jnp.where(lax.broadcasted_iota(jnp.int32, (nbp, tq), 1) >= blk, 1.0, 0.0)
        past = blk_id < float(2 * t) + second

        def choose(qm):
            cur = jnp.where(past, _dot_nt(km_hi, qm) + _dot_nt(km_lo, qm), NEG)
            sel = jnp.zeros_like(cur)
            for _ in range(MOBA_TOPK):
                mx = jnp.max(cur, axis=0, keepdims=True)
                first = jnp.min(jnp.where(cur == mx, blk_id, float(nbp)), axis=0, keepdims=True)
                hit = blk_id == first
                sel = jnp.where(hit & past, 1.0, sel)
                cur = jnp.where(hit, -jnp.inf, cur)
            return jnp.where(sel > 0.0, 0.0, NEG)

        fill = jnp.zeros((half - nbp, tq), F32)
        bias = jnp.concatenate([choose(q_own[1]), fill, choose(q_own[0]), fill], axis=0).T
        qown_ref[0] = q_own[0]
        qown_ref[1] = q_own[1]
        qpast_ref[0] = jnp.where(head0, q2, bias).astype(BF16)
        qpast_ref[1] = jnp.where(head0, bias, q2).astype(BF16)

    km_ref[...] = jnp.zeros_like(km_ref)
    lane_k = lax.broadcasted_iota(jnp.int32, (blk, LANES), 1)
    h0 = lane_k < half
    one = jnp.ones((blk, LANES), F32)
    zero_k = jnp.zeros((blk, LANES), F32)
    for j in range(nb):
        at = pl.ds(j * blk, blk)
        kb = k_ref[at, :].astype(F32)
        vb = v_ref[at, :]
        km_ref[pl.ds(j, 1), :] = jnp.sum(kb, axis=0, keepdims=True) * (1.0 / blk)
        cols_j = pl.ds((j % 2) * blk, blk)
        kidt_ref[0, j // 2, :, cols_j] = jnp.where(
            h0, kb, jnp.where(lane_k == half + j, one, zero_k)).T.astype(BF16)
        kidt_ref[1, j // 2, :, cols_j] = jnp.where(
            h0, jnp.where(lane_k == j, one, zero_k), kb).T.astype(BF16)
        vone_ref[0, at, :] = jnp.where(h0, vb, one.astype(BF16))
        vone_ref[1, at, :] = jnp.where(h0, one.astype(BF16), vb)

    rows = lax.broadcasted_iota(jnp.int32, (blk, blk), 0)
    cols = lax.broadcasted_iota(jnp.int32, (blk, blk), 1)
    causal = cols <= rows
    halves = (pl.ds(0, blk), pl.ds(blk, blk))
    same_head = ((lax.broadcasted_iota(jnp.int32, (LANES, LANES), 0) < half)
                 == (lax.broadcasted_iota(jnp.int32, (LANES, LANES), 1) < half))
    avg = jnp.where(same_head, 1.0 / HEAD_DIM, 0.0).astype(BF16)

    def finish_tile(t):
        acc = (acc_ref[0], acc_ref[1])
        row_sum = pltpu.roll(jnp.where(head0, acc[1], acc[0]), half, 1)
        o = jnp.where(head0, acc[0], acc[1]) / row_sum
        sq_hi, sq_lo = _split_bf16(o * o)
        ms = _dot(sq_hi, avg) + _dot(sq_lo, avg)
        o_ref[pl.ds(t * tq, tq), :] = (o * lax.rsqrt(ms + EPS) * g_ref[...]).astype(BF16)

    def query_tile(t):
        q_own = (qown_ref[0], qown_ref[1])
        q_past = (qpast_ref[0], qpast_ref[1])

        row_a = pl.ds(t * tq, blk)
        own = pl.ds(t * tq, tq)
        for h in range(2):
            kt = kidt_ref[h, t]
            s_a = jnp.where(causal, _dot(q_own[h][:blk], kt[:, :blk]), NEG)
            s_b = jnp.concatenate([_dot(q_past[h][blk:], kt[:, :blk]),
                                   jnp.where(causal, _dot(q_own[h][blk:], kt[:, blk:]), NEG)], axis=1)
            sda_ref[h] = s_a
            sdb_ref[h] = s_b
            mrun_ref[h, halves[0], :] = _fold_lanes(s_a, jnp.maximum)
            mrun_ref[h, halves[1], :] = _fold_lanes(s_b, jnp.maximum)

        for h in range(2):
            best = mrun_ref[h]
            for jj in range(t):
                s = _dot(q_past[h], kidt_ref[h, jj])
                s_ref[h, jj] = s
                best = jnp.maximum(best, _fold_lanes(s, jnp.maximum))
            mrun_ref[h] = best

        m = [[jnp.max(mrun_ref[h, r, :], axis=-1, keepdims=True) for r in halves] for h in range(2)]
        for h in range(2):
            p_a = jnp.exp2(sda_ref[h] - m[h][0])
            p_b = jnp.exp2(sdb_ref[h] - m[h][1])
            acc_ref[h, halves[0], :] = _dot(p_a.astype(BF16), vone_ref[h, row_a, :])
            acc_ref[h, halves[1], :] = _dot(p_b.astype(BF16), vone_ref[h, own, :])

        for h in range(2):
            if t:
                p = jnp.concatenate([
                    jnp.concatenate([jnp.exp2(s_ref[h, jj, r, :] - m[h][e])
                                     for e, r in enumerate(halves)], axis=0).astype(BF16)
                    for jj in range(t)], axis=1)
                acc_ref[h] += _dot(p, vone_ref[h, pl.ds(0, t * tq), :])

    for t in range(nt):
        prepare_queries(t)
        query_tile(t)
        finish_tile(t)


def _moba(qa, ka, va, g_pairs):
    bsz, s, w = qa.shape
    blk = MOBA_BLOCK
    tq = 2 * blk
    assert s % tq == 0 and w % LANES == 0
    nb = s // blk
    nbp = -(-nb // 8) * 8
    assert nbp <= LANES // 2
    npairs = w // LANES
    seq = pl.BlockSpec((None, s, LANES), lambda b, p: (b, 0, p))
    return pl.pallas_call(
        functools.partial(_moba_kernel, nb=nb),
        grid=(bsz, npairs),
        in_specs=[seq, seq, seq, pl.BlockSpec((None, 1, LANES), lambda b, p: (p, 0, 0))],
        out_specs=seq,
        out_shape=jax.ShapeDtypeStruct((bsz, s, w), BF16),
        scratch_shapes=[
            pltpu.VMEM((nbp, LANES), F32),
            pltpu.VMEM((2, s // tq, LANES, tq), BF16),
            pltpu.VMEM((2, s, LANES), BF16),
            pltpu.VMEM((2, tq, LANES), BF16),
            pltpu.VMEM((2, tq, LANES), BF16),
            pltpu.VMEM((2, max(s // tq - 1, 1), tq, tq), F32),
            pltpu.VMEM((2, blk, blk), F32),
            pltpu.VMEM((2, blk, tq), F32),
            pltpu.VMEM((2, tq, LANES), F32),
            pltpu.VMEM((2, tq, LANES), F32),
        ],
        compiler_params=_params("parallel", "parallel"),
        name="moba",
    )(qa, ka, va, g_pairs)


def _gla_kernel(q_ref, k_ref, la_ref, v_ref, gg_ref, g_ref, o_ref):
    c = GLA_CHUNK
    dv = GLA_DV
    half = LANES // 2
    tr = min(GLA_ROWS, q_ref.shape[0])
    shift = c.bit_length() - 1
    grp = min(tr, GLA_SUM_ROWS)
    rows = lax.broadcasted_iota(jnp.int32, (grp, grp), 0)
    cols = lax.broadcasted_iota(jnp.int32, (grp, grp), 1)
    same_chunk = lax.shift_right_logical(rows, shift) == lax.shift_right_logical(cols, shift)
    upto = jnp.where(same_chunk & (cols <= rows), 1.0, 0.0).astype(BF16)
    head0 = lax.broadcasted_iota(jnp.int32, (tr, LANES), 1) < half
    in_chunk_row = lax.broadcasted_iota(jnp.int32, (2 * c, c), 0) & (c - 1)
    causal = lax.broadcasted_iota(jnp.int32, (2 * c, c), 1) <= in_chunk_row

    def row_tile(t, st):
        r0 = t * tr
        at_tile = pl.ds(r0, tr)
        b = []
        for g0 in range(0, tr, grp):
            la_hi, la_lo = _split_bf16(la_ref[pl.ds(r0 + g0, grp), :])
            b.append(_dot(upto, la_hi) + _dot(upto, la_lo))
        b = jnp.concatenate(b, axis=0)
        b_end = jnp.broadcast_to(b.reshape(tr // c, c, LANES)[:, c - 1:c, :],
                                 (tr // c, c, LANES)).reshape(tr, LANES)
        kk = k_ref[at_tile, :]
        q_dec = q_ref[at_tile, :] * jnp.exp(b)
        k_dec = (kk * jnp.exp(-b)).astype(BF16)
        k_end = (kk * jnp.exp(b_end - b)).astype(BF16)
        decay = jnp.exp(b_end)
        zero = jnp.zeros_like(q_dec)
        q_h0 = jnp.where(head0, q_dec, zero).astype(BF16)
        q_h1 = jnp.where(head0, zero, q_dec).astype(BF16)

        outs = []
        for n in range(tr // c):
            at = slice(n * c, (n + 1) * c)
            qs = jnp.concatenate([q_h0[at], q_h1[at]], axis=0)
            attn = jnp.where(causal, _dot_nt(qs, k_dec[at]), 0.0)
            v_c = v_ref[pl.ds(r0 + n * c, c), :]
            res = _dot(attn.astype(BF16), v_c) + _dot_nt(qs, st.astype(BF16))
            outs.append(jnp.concatenate([res[:c, :dv], res[c:, dv:]], axis=1))
            st = st * decay[n * c:n * c + 1, :] + _dot_tn(v_c, k_end[at])
        o = jnp.concatenate(outs, axis=0)

        for h in range(2):
            head = slice(h * dv, (h + 1) * dv)
            o_h = o[:, head]
            ms = jnp.mean(o_h * o_h, axis=-1, keepdims=True)
            gate = gg_ref[at_tile, head].astype(F32)
            y = o_h * lax.rsqrt(ms + EPS) * g_ref[:, head] * jax.nn.silu(gate)
            o_ref[at_tile, head] = y.astype(BF16)
        return st

    st = jnp.zeros((2 * dv, LANES), F32)
    for t in range(q_ref.shape[0] // tr):
        st = row_tile(t, st)


def _gla(qg, kg, la, vg, gg, g_pairs):
    bsz, s, kw = qg.shape
    vw = vg.shape[-1]
    tr = min(GLA_ROWS, s)
    assert s % tr == 0 and tr % min(tr, GLA_SUM_ROWS) == 0 and tr % GLA_CHUNK == 0
    assert GLA_CHUNK & (GLA_CHUNK - 1) == 0
    npairs = kw // LANES
    kspec = pl.BlockSpec((None, s, LANES), lambda b, p: (b, 0, p))
    vspec = pl.BlockSpec((None, s, 2 * GLA_DV), lambda b, p: (b, 0, p))
    return pl.pallas_call(
        _gla_kernel,
        grid=(bsz, npairs),
        in_specs=[kspec, kspec, kspec, vspec, vspec,
                  pl.BlockSpec((None, 1, 2 * GLA_DV), lambda b, p: (p, 0, 0))],
        out_specs=vspec,
        out_shape=jax.ShapeDtypeStruct((bsz, s, vw), BF16),
        compiler_params=_params("parallel", "parallel"),
        name="gla",
    )(qg, kg, la, vg, gg, g_pairs)


def _mix_ffn_kernel(oa_ref, oah_ref, og_ref, ogh_ref, x_ref, xh_ref, mod_ref, woa_ref, wog_ref,
                    g1_ref, b1_ref, wup_ref, bup_ref, cw_ref, cb_ref, wd_ref, g2_ref, b2_ref,
                    o_ref, x1_ref, a_ref):
    tm = x_ref.shape[0]
    halo = xh_ref.shape[0]
    d_ff = wd_ref.shape[0]
    mod = mod_ref[...]
    oa = jnp.concatenate([oah_ref[...], oa_ref[...]], axis=0)
    og = jnp.concatenate([ogh_ref[...], og_ref[...]], axis=0)
    xc = jnp.concatenate([xh_ref[...], x_ref[...]], axis=0)
    y = _dot(oa, woa_ref[...]) + _dot(og, wog_ref[...])
    x1 = _layer_norm(ALPHA * xc + mod[2:3] * y) * g1_ref[...] + b1_ref[...]
    ucat = (_layer_norm(x1) * (1.0 + mod[4:5]) + mod[3:4]).astype(BF16)
    x1_ref[...] = x1[halo:]
    u = ucat[halo:]
    rowid = lax.broadcasted_iota(jnp.int32, (halo + tm, 1), 0)
    keep = (rowid >= halo) | (pl.program_id(1) > 0)
    for c0 in range(0, d_ff, FF_CHUNK):
        val = slice(c0, c0 + FF_CHUNK)
        gate = slice(d_ff + c0, d_ff + c0 + FF_CHUNK)
        hv = _dot(u, wup_ref[:, val]) + bup_ref[:, val]
        hg = jnp.where(keep, _dot(ucat, wup_ref[:, gate]) + bup_ref[:, gate], 0.0)
        conv = cb_ref[:, val]
        for tap in range(CONV_WIDTH):
            back = CONV_WIDTH - 1 - tap
            conv = conv + cw_ref[tap:tap + 1, val] * hg[halo - back:halo - back + tm]
        a_ref[:, val] = (0.5 * conv * (1.0 + lax.erf(conv * (0.5 ** 0.5))) * hv).astype(BF16)
    h2 = ALPHA * x1_ref[...] + mod[5:6] * _dot(a_ref[...], wd_ref[...])
    o_ref[...] = _layer_norm(h2) * g2_ref[...] + b2_ref[...]


def _mix_ffn(oa, og, x, mod, w_oa, w_og, ln1_g, ln1_b, w_up, b_up, conv_w, conv_b, w_d, ln2_g, ln2_b):
    bsz, s, d = x.shape
    d_ff = w_d.shape[0]
    tm = min(FFN_ROWS, s)
    halo = CONV_HALO
    assert s % tm == 0 and tm % halo == 0 and halo >= CONV_WIDTH - 1 and d_ff % FF_CHUNK == 0
    row = lambda b, i: (b, i, 0)
    prev = lambda b, i: (b, jnp.maximum(i * (tm // halo) - 1, 0), 0)
    const = lambda b, i: (0, 0)
    tile = lambda a: [pl.BlockSpec((None, tm, a.shape[-1]), row),
                      pl.BlockSpec((None, halo, a.shape[-1]), prev)]
    consts = (w_oa, w_og, ln1_g, ln1_b, w_up, b_up, conv_w, conv_b, w_d, ln2_g, ln2_b)
    return pl.pallas_call(
        _mix_ffn_kernel,
        grid=(bsz, s // tm),
        in_specs=tile(oa) + tile(og) + tile(x) + [pl.BlockSpec((None, 6, d), lambda b, i: (b, 0, 0))]
        + [pl.BlockSpec(a.shape, const) for a in consts],
        out_specs=pl.BlockSpec((None, tm, d), row),
        out_shape=jax.ShapeDtypeStruct((bsz, s, d), F32),
        scratch_shapes=[pltpu.VMEM((tm, d), F32), pltpu.VMEM((tm, d_ff), BF16)],
        compiler_params=_params("parallel", "parallel"),
        name="mix_ffn",
    )(oa, oa, og, og, x, x, mod, *consts)


def _layer(x, mod, w_in, w_gla_gate, b_gla_gate, attn_norm_g, gla_norm_g, w_o, ln1_g, ln1_b,
           w_up, b_up, conv_w, conv_b, w_down, ln2_g, ln2_b):
    d = x.shape[-1]
    d_ff = w_down.shape[0]
    a3 = 3 * ATTN_WIDTH
    g_end = a3 + 2 * GLA_K_WIDTH + 2 * GLA_V_WIDTH
    w_a = w_in[:, :a3].astype(BF16)
    w_g = w_in[:, a3:g_end].astype(BF16)
    w_r = jnp.pad(w_in[:, g_end:], ((0, 0), (0, LANES - GLA_GATE_RANK))).astype(BF16)
    w_gate = jnp.pad(w_gla_gate, ((0, LANES - GLA_GATE_RANK), (0, 0)))
    qa, ka, va, qg, kg, vg, gg, la = _in_proj(x, mod, w_a, w_g, w_r, w_gate,
                                              b_gla_gate.reshape(1, -1))
    oa = _moba(qa, ka, va, attn_norm_g.reshape(-1, 1, LANES))
    og = _gla(qg, kg, la, vg, gg, gla_norm_g.reshape(-1, 1, 2 * GLA_DV))
    w_ob = w_o.astype(BF16)
    return _mix_ffn(oa, og, x, mod, w_ob[:ATTN_WIDTH], w_ob[ATTN_WIDTH:],
                    ln1_g.reshape(1, d), ln1_b.reshape(1, d), w_up.astype(BF16), b_up.reshape(1, -1),
                    conv_w, conv_b.reshape(1, -1), w_down.astype(BF16),
                    ln2_g.reshape(1, d), ln2_b.reshape(1, d))


def kernel(x, c, w_ada, b_ada, w_in, w_gla_gate, b_gla_gate, attn_norm_g, gla_norm_g, w_o,
           ln1_g, ln1_b, w_up, b_up, conv_w, conv_b, w_down, ln2_g, ln2_b):
    bsz, _, d = x.shape
    for l in range(w_in.shape[0]):
        mod = _adaln_mod(c, w_ada[l], b_ada[l]).reshape(bsz, 6, d)
        x = _layer(x, mod, w_in[l], w_gla_gate[l], b_gla_gate[l], attn_norm_g[l], gla_norm_g[l],
                   w_o[l], ln1_g[l], ln1_b[l], w_up[l], b_up[l], conv_w[l], conv_b[l], w_down[l],
                   ln2_g[l], ln2_b[l])
    return x
```
